```python
import jax, jax.numpy as jnp
from jax import lax
import numpy as np

D_MODEL = 2048
BATCH = 2
SEQ = 8192
DEPTH = 2

N_MIXERS = 2
N_RET_LAYERS = (DEPTH + N_MIXERS - 1) // N_MIXERS
N_ATT_LAYERS = DEPTH // N_MIXERS

RET_HEADS = 8
RET_DK = D_MODEL // RET_HEADS
RET_DV = 2 * RET_DK
RET_QK_WIDTH = RET_HEADS * RET_DK
RET_V_WIDTH = RET_HEADS * RET_DV
RET_IN_WIDTH = 2 * RET_QK_WIDTH + 2 * RET_V_WIDTH
RET_CHUNK = 128
RET_ROT_BASE = 10000.0

ATT_GROUPS = ((128, 1), (512, 4), (2048, 16))
N_ATT_GROUPS = len(ATT_GROUPS)
ATT_HEAD_DIM = 128
ATT_HEADS = D_MODEL // ATT_HEAD_DIM
ATT_WIDTH = ATT_HEADS * ATT_HEAD_DIM
ATT_IN_WIDTH = 3 * N_ATT_GROUPS * ATT_WIDTH
ATT_BLOCK = 128
ROPE_DIM = ATT_HEAD_DIM // 4
ROPE_THETA = 500000.0

D_FF = 4 * D_MODEL
EPS = 1e-6

kernel_name = "hybrid_retention_dilated_attention_block"


def rms_norm(x, gain):
    x32 = x.astype(jnp.float32)
    y = x32 * lax.rsqrt(jnp.mean(x32 * x32, axis=-1, keepdims=True) + EPS)
    return (y * gain.astype(jnp.float32)).astype(x.dtype)


def head_rms(x):
    return x * lax.rsqrt(jnp.mean(x * x, axis=-1, keepdims=True) + EPS)


def rotate(x, pos, inv_freq):
    half = inv_freq.shape[0]
    ang = pos[:, None] * inv_freq[None, :]
    cos = jnp.cos(ang)[None, :, None, :]
    sin = jnp.sin(ang)[None, :, None, :]
    x1 = x[..., :half]
    x2 = x[..., half:2 * half]
    rest = x[..., 2 * half:]
    return jnp.concatenate([x1 * cos - x2 * sin, x2 * cos + x1 * sin, rest], axis=-1)


def retention_mixer(h, w_in, w_out, pos):
    B, S, _ = h.shape
    n_chunks = S // RET_CHUNK
    proj = (h @ w_in).astype(jnp.float32)
    q, k, v, g = jnp.split(proj, [RET_QK_WIDTH, 2 * RET_QK_WIDTH, 2 * RET_QK_WIDTH + RET_V_WIDTH], axis=-1)
    inv_freq = 1.0 / (RET_ROT_BASE ** jnp.linspace(0.0, 1.0, RET_DK // 2, dtype=jnp.float32))
    q = rotate(q.reshape(B, S, RET_HEADS, RET_DK), pos, inv_freq)
    k = rotate(k.reshape(B, S, RET_HEADS, RET_DK), pos, inv_freq) * (RET_DK ** -0.5)
    v = v.reshape(B, S, RET_HEADS, RET_DV)

    log_gamma = jnp.log(1.0 - 2.0 ** (-5.0 - jnp.arange(RET_HEADS, dtype=jnp.float32)))
    idx = jnp.arange(RET_CHUNK, dtype=jnp.float32)
    diff = idx[:, None] - idx[None, :]
    inner_decay = jnp.where(diff >= 0, jnp.exp(log_gamma[:, None, None] * jnp.maximum(diff, 0.0)), 0.0)
    query_decay = jnp.exp(log_gamma[None, :] * (idx[:, None] + 1.0))
    key_decay = jnp.exp(log_gamma[None, :] * (RET_CHUNK - 1.0 - idx[:, None]))
    chunk_decay = jnp.exp(log_gamma * RET_CHUNK)

    def to_chunks(t):
        return t.reshape(B, n_chunks, RET_CHUNK, *t.shape[2:]).swapaxes(0, 1)

    def step(state, qkv):
        qc, kc, vc = qkv
        scores = jnp.einsum('bihd,bjhd->bhij', qc, kc) * inner_decay
        inner = jnp.einsum('bhij,bjhv->bihv', scores, vc)
        cross = jnp.einsum('bihd,bhdv->bihv', qc, state) * query_decay[None, :, :, None]
        state = state * chunk_decay[None, :, None, None] + jnp.einsum(
            'bjhd,bjhv->bhdv', kc * key_decay[None, :, :, None], vc)
        return state, inner + cross

    state0 = jnp.zeros((B, RET_HEADS, RET_DK, RET_DV), jnp.float32)
    _, out = lax.scan(step, state0, (to_chunks(q), to_chunks(k), to_chunks(v)))
    out = out.swapaxes(0, 1).reshape(B, S, RET_HEADS, RET_DV)
    out = head_rms(out).reshape(B, S, RET_V_WIDTH) * jax.nn.silu(g)
    return out.astype(h.dtype) @ w_out


def dilated_window_attention(q, k, v, dilation, steps):
    B, S, H, Dh = q.shape
    L = S // dilation
    n_blk = -(-L // ATT_BLOCK)
    Lp = n_blk * ATT_BLOCK

    def to_blocks(t):
        t = t.reshape(B, L, dilation, H, Dh).swapaxes(1, 2).reshape(B * dilation, L, H, Dh)
        t = jnp.pad(t, ((0, 0), (0, Lp - L), (0, 0), (0, 0)))
        return t.reshape(B * dilation, n_blk, ATT_BLOCK, H, Dh)

    def with_prev(t):
        prev = jnp.pad(t, ((0, 0), (1, 0), (0, 0), (0, 0), (0, 0)))[:, :-1]
        return jnp.concatenate([prev, t], axis=2)

    qb = to_blocks(q)
    kw = with_prev(to_blocks(k))
    vw = with_prev(to_blocks(v))

    qi = jnp.arange(ATT_BLOCK)[:, None]
    kj = jnp.arange(2 * ATT_BLOCK)[None, :]
    dist = ATT_BLOCK + qi - kj
    blk = jnp.arange(n_blk)[:, None, None]
    valid = (dist >= 0) & (dist <= steps) & (blk * ATT_BLOCK - ATT_BLOCK + kj >= 0)

    s = jnp.einsum('znqhd,znkhd->znhqk', qb, kw)
    s = jnp.where(valid[None, :, None], s, -jnp.inf)
    m = jnp.max(s, axis=-1, keepdims=True)
    p = jnp.exp(s - m)
    denom = jnp.sum(p, axis=-1, keepdims=True)
    o = jnp.einsum('znhqk,znkhd->znqhd', p, vw) / jnp.swapaxes(denom, 2, 3)
    lse = jnp.swapaxes((m + jnp.log(denom))[..., 0], 2, 3)

    o = o.reshape(B, dilation, Lp, H, Dh)[:, :, :L].swapaxes(1, 2).reshape(B, S, H, Dh)
    lse = lse.reshape(B, dilation, Lp, H)[:, :, :L].swapaxes(1, 2).reshape(B, S, H)
    return o, lse


def dilated_attention_mixer(h, w_in, q_gain, k_gain, w_out, pos):
    B, S, _ = h.shape
    proj = (h @ w_in).astype(jnp.float32).reshape(B, S, 3, N_ATT_GROUPS, ATT_HEADS, ATT_HEAD_DIM)
    inv_freq = ROPE_THETA ** (-jnp.arange(0, ROPE_DIM, 2, dtype=jnp.float32) / ROPE_DIM)
    all_heads = N_ATT_GROUPS * ATT_HEADS
    q = proj[:, :, 0].reshape(B, S, all_heads, ATT_HEAD_DIM)
    k = proj[:, :, 1].reshape(B, S, all_heads, ATT_HEAD_DIM)
    v = proj[:, :, 2]
    q = rotate(head_rms(q) * q_gain.astype(jnp.float32), pos, inv_freq) * (ATT_HEAD_DIM ** -0.5)
    k = rotate(head_rms(k) * k_gain.astype(jnp.float32), pos, inv_freq)
    q = q.reshape(B, S, N_ATT_GROUPS, ATT_HEADS, ATT_HEAD_DIM)
    k = k.reshape(B, S, N_ATT_GROUPS, ATT_HEADS, ATT_HEAD_DIM)

    outs, lses = [], []
    for g, (window, dilation) in enumerate(ATT_GROUPS):
        o, l = dilated_window_attention(q[:, :, g], k[:, :, g], v[:, :, g], dilation, window // dilation)
        outs.append(o)
        lses.append(l)
    weights = jax.nn.softmax(jnp.stack(lses, axis=0), axis=0)
    out = jnp.sum(weights[..., None] * jnp.stack(outs, axis=0), axis=0)
    return out.reshape(B, S, ATT_WIDTH).astype(h.dtype) @ w_out


def squared_relu_mlp(h, w_in, w_out):
    return jnp.square(jax.nn.relu(h @ w_in)) @ w_out


def setup_inputs(seed: int = 0) -> dict:
    key = jax.random.key(seed)
    ks = jax.random.split(key, 11)
    f32 = jnp.float32
    x = jax.random.normal(ks[0], (BATCH, SEQ, D_MODEL), f32)
    norm_mix_gain = 1.0 + 0.02 * jax.random.normal(ks[1], (DEPTH, D_MODEL), f32)
    norm_mlp_gain = 1.0 + 0.02 * jax.random.normal(ks[2], (DEPTH, D_MODEL), f32)
    ret_w_in = jax.random.normal(ks[3], (N_RET_LAYERS, D_MODEL, RET_IN_WIDTH), f32) * D_MODEL ** -0.5
    ret_w_out = jax.random.normal(ks[4], (N_RET_LAYERS, RET_V_WIDTH, D_MODEL), f32) * RET_V_WIDTH ** -0.5
    att_w_in = jax.random.normal(ks[5], (N_ATT_LAYERS, D_MODEL, ATT_IN_WIDTH), f32) * D_MODEL ** -0.5
    att_q_gain = 1.0 + 0.02 * jax.random.normal(ks[6], (N_ATT_LAYERS, ATT_HEAD_DIM), f32)
    att_k_gain = 1.0 + 0.02 * jax.random.normal(ks[7], (N_ATT_LAYERS, ATT_HEAD_DIM), f32)
    att_w_out = jax.random.normal(ks[8], (N_ATT_LAYERS, ATT_WIDTH, D_MODEL), f32) * ATT_WIDTH ** -0.5
    mlp_w_in = jax.random.normal(ks[9], (DEPTH, D_MODEL, D_FF), f32) * D_MODEL ** -0.5
    mlp_w_out = jax.random.normal(ks[10], (DEPTH, D_FF, D_MODEL), f32) * D_FF ** -0.5
    return {"x": x, "norm_mix_gain": norm_mix_gain, "norm_mlp_gain": norm_mlp_gain,
            "ret_w_in": ret_w_in, "ret_w_out": ret_w_out,
            "att_w_in": att_w_in, "att_q_gain": att_q_gain, "att_k_gain": att_k_gain,
            "att_w_out": att_w_out, "mlp_w_in": mlp_w_in, "mlp_w_out": mlp_w_out}


def reference(x, norm_mix_gain, norm_mlp_gain, ret_w_in, ret_w_out, att_w_in, att_q_gain,
              att_k_gain, att_w_out, mlp_w_in, mlp_w_out):
    S = x.shape[1]
    pos = jnp.arange(S, dtype=jnp.float32)
    h = x
    for i in range(DEPTH):
        hn = rms_norm(h, norm_mix_gain[i])
        j = i // N_MIXERS
        if i % N_MIXERS == 0:
            mix = retention_mixer(hn, ret_w_in[j], ret_w_out[j], pos)
        else:
            mix = dilated_attention_mixer(hn, att_w_in[j], att_q_gain[j], att_k_gain[j], att_w_out[j], pos)
        h = h + mix
        hn = rms_norm(h, norm_mlp_gain[i])
        h = h + squared_relu_mlp(hn, mlp_w_in[i], mlp_w_out[i])
    return h
```

```python
import functools

import jax
import jax.numpy as jnp
from jax import lax
from jax.experimental import pallas as pl
from jax.experimental.pallas import tpu as pltpu

F32 = jnp.float32
BF16 = jnp.bfloat16

D_MODEL = 2048
BATCH = 2
SEQ = 8192
TOKENS = BATCH * SEQ
EPS = 1e-6

RET_HEADS = 8
RET_DK = 256
RET_DV = 512
RET_QK_WIDTH = RET_HEADS * RET_DK
RET_V_WIDTH = RET_HEADS * RET_DV
RET_IN_WIDTH = 2 * RET_QK_WIDTH + 2 * RET_V_WIDTH
RET_CHUNK = 128
RET_ROT_BASE = 10000.0

ATT_GROUPS = ((128, 1), (512, 4), (2048, 16))
N_ATT_GROUPS = 3
ATT_HEAD_DIM = 128
ATT_HEADS = 16
ATT_WIDTH = ATT_HEADS * ATT_HEAD_DIM
ATT_IN_WIDTH = 3 * N_ATT_GROUPS * ATT_WIDTH
ATT_BLOCK = 128
ROPE_DIM = 32
ROPE_HALF = ROPE_DIM // 2
ROPE_THETA = 500000.0

D_FF = 4 * D_MODEL

LANES = 128
VMEM_LIMIT = 56 * 1024 * 1024


def _params(semantics):
    return pltpu.CompilerParams(dimension_semantics=semantics, vmem_limit_bytes=VMEM_LIMIT)


def _rms_norm_bf16(x, gain):
    ms = jnp.mean(x * x, axis=-1, keepdims=True)
    return (x * lax.rsqrt(ms + EPS) * gain).astype(BF16)


def _ret_inproj_kernel(h_ref, gain_ref, w_ref, cos_ref, sin_ref, o_ref, hn_ref, *,
                       n_q_tiles, n_rot_tiles, heads_per_tile):
    j = pl.program_id(1)

    @pl.when(j == 0)
    def _():
        hn_ref[...] = _rms_norm_bf16(h_ref[...], gain_ref[...])

    acc = jnp.dot(hn_ref[...], w_ref[...], preferred_element_type=F32)

    @pl.when(j >= n_rot_tiles)
    def _():
        o_ref[...] = acc.astype(o_ref.dtype)

    @pl.when(j < n_rot_tiles)
    def _():
        scale = jnp.where(j >= n_q_tiles, RET_DK ** -0.5, 1.0).astype(F32)
        c = cos_ref[...]
        s = sin_ref[...]
        half = RET_DK // 2
        for hh in range(heads_per_tile):
            lo = hh * RET_DK
            x1 = acc[:, lo:lo + half]
            x2 = acc[:, lo + half:lo + RET_DK]
            o_ref[:, lo:lo + half] = ((x1 * c - x2 * s) * scale).astype(o_ref.dtype)
            o_ref[:, lo + half:lo + RET_DK] = ((x2 * c + x1 * s) * scale).astype(o_ref.dtype)


def _ret_inproj(h, gain, w, cos, sin, *, tm=1024, tn=1024):
    n_pos_blocks = SEQ // tm
    kern = functools.partial(
        _ret_inproj_kernel, n_q_tiles=RET_QK_WIDTH // tn, n_rot_tiles=2 * RET_QK_WIDTH // tn,
        heads_per_tile=tn // RET_DK)
    return pl.pallas_call(
        kern,
        grid=(TOKENS // tm, RET_IN_WIDTH // tn),
        in_specs=[
            pl.BlockSpec((tm, D_MODEL), lambda i, j: (i, 0)),
            pl.BlockSpec((1, D_MODEL), lambda i, j: (0, 0)),
            pl.BlockSpec((D_MODEL, tn), lambda i, j: (0, j)),
            pl.BlockSpec((tm, RET_DK // 2), lambda i, j: (i % n_pos_blocks, 0)),
            pl.BlockSpec((tm, RET_DK // 2), lambda i, j: (i % n_pos_blocks, 0)),
        ],
        out_specs=pl.BlockSpec((tm, tn), lambda i, j: (i, j)),
        out_shape=jax.ShapeDtypeStruct((TOKENS, RET_IN_WIDTH), BF16),
        scratch_shapes=[pltpu.VMEM((tm, D_MODEL), BF16)],
        compiler_params=_params(("parallel", "arbitrary")),
        name="ret_inproj",
    )(h, gain, w, cos, sin)


def _retention_kernel(q_ref, k_ref, v_ref, g_ref, dec_ref, qd_ref, kd_ref, cd_ref, o_ref,
                      state_ref, *, n_chunks):
    @pl.when(pl.program_id(2) == 0)
    def _():
        state_ref[...] = jnp.zeros_like(state_ref)

    dec = dec_ref[...]
    qd = qd_ref[...]
    kd = kd_ref[...]
    cd = cd_ref[...]

    def body(c, carry):
        rows = pl.ds(pl.multiple_of(c * RET_CHUNK, RET_CHUNK), RET_CHUNK)
        qc = q_ref[rows, :]
        kc = k_ref[rows, :]
        vc = v_ref[rows, :]
        state = state_ref[...]
        scores = lax.dot_general(qc, kc, (((1,), (1,)), ((), ())),
                                 preferred_element_type=F32) * dec
        inner = jnp.dot(scores.astype(BF16), vc, preferred_element_type=F32)
        cross = jnp.dot(qc, state.astype(BF16), preferred_element_type=F32) * qd
        k_dec = (kc.astype(F32) * kd).astype(BF16)
        update = lax.dot_general(k_dec, vc, (((0,), (0,)), ((), ())),
                                 preferred_element_type=F32)
        state_ref[...] = state * cd + update
        out = inner + cross
        ms = jnp.mean(out * out, axis=-1, keepdims=True)
        gate = g_ref[rows, :].astype(F32)
        y = out * lax.rsqrt(ms + EPS) * (gate * jax.nn.sigmoid(gate))
        o_ref[rows, :] = y.astype(o_ref.dtype)
        return carry

    lax.fori_loop(0, n_chunks, body, 0)


def _retention(proj, dec, qd, kd, cd, *, rows=1024):
    n_row_blocks = SEQ // rows
    k_off = RET_QK_WIDTH // RET_DK
    v_off = 2 * RET_QK_WIDTH // RET_DV
    g_off = v_off + RET_V_WIDTH // RET_DV
    row = lambda b, h, r: b * n_row_blocks + r
    kern = functools.partial(_retention_kernel, n_chunks=rows // RET_CHUNK)
    return pl.pallas_call(
        kern,
        grid=(BATCH, RET_HEADS, n_row_blocks),
        in_specs=[
            pl.BlockSpec((rows, RET_DK), lambda b, h, r: (row(b, h, r), h)),
            pl.BlockSpec((rows, RET_DK), lambda b, h, r: (row(b, h, r), k_off + h)),
            pl.BlockSpec((rows, RET_DV), lambda b, h, r: (row(b, h, r), v_off + h)),
            pl.BlockSpec((rows, RET_DV), lambda b, h, r: (row(b, h, r), g_off + h)),
            pl.BlockSpec((None, RET_CHUNK, RET_CHUNK), lambda b, h, r: (h, 0, 0)),
            pl.BlockSpec((None, RET_CHUNK, 1), lambda b, h, r: (h, 0, 0)),
            pl.BlockSpec((None, RET_CHUNK, 1), lambda b, h, r: (h, 0, 0)),
            pl.BlockSpec((None, 1, RET_DV), lambda b, h, r: (h, 0, 0)),
        ],
        out_specs=pl.BlockSpec((rows, RET_DV), lambda b, h, r: (row(b, h, r), h)),
        out_shape=jax.ShapeDtypeStruct((TOKENS, RET_V_WIDTH), BF16),
        scratch_shapes=[pltpu.VMEM((RET_DK, RET_DV), F32)],
        compiler_params=_params(("parallel", "parallel", "arbitrary")),
        name="retention",
    )(proj, proj, proj, proj, dec, qd, kd, cd)


def _outproj_kernel(y_ref, w_ref, res_ref, o_ref):
    o_ref[...] = res_ref[...] + jnp.dot(y_ref[...], w_ref[...], preferred_element_type=F32)


def _outproj(y, w, res, *, tm=512, tn=1024):
    k = y.shape[1]
    return pl.pallas_call(
        _outproj_kernel,
        grid=(D_MODEL // tn, TOKENS // tm),
        in_specs=[
            pl.BlockSpec((tm, k), lambda j, i: (i, 0)),
            pl.BlockSpec((k, tn), lambda j, i: (0, j)),
            pl.BlockSpec((tm, tn), lambda j, i: (i, j)),
        ],
        out_specs=pl.BlockSpec((tm, tn), lambda j, i: (i, j)),
        out_shape=jax.ShapeDtypeStruct((TOKENS, D_MODEL), F32),
        compiler_params=_params(("parallel", "parallel")),
        name="outproj",
    )(y, w, res)


def _mlp_kernel(h_ref, gain_ref, w1_ref, w2_ref, o_ref, hn_ref):
    @pl.when(pl.program_id(1) == 0)
    def _():
        h = h_ref[...]
        hn_ref[...] = _rms_norm_bf16(h, gain_ref[...])
        o_ref[...] = h

    a = jnp.dot(hn_ref[...], w1_ref[...], preferred_element_type=F32)
    act = jnp.square(jnp.maximum(a, 0.0)).astype(BF16)
    o_ref[...] += jnp.dot(act, w2_ref[...], preferred_element_type=F32)


def _mlp(h, gain, w1, w2, *, tm=512, tf=1024):
    return pl.pallas_call(
        _mlp_kernel,
        grid=(TOKENS // tm, D_FF // tf),
        in_specs=[
            pl.BlockSpec((tm, D_MODEL), lambda i, f: (i, 0)),
            pl.BlockSpec((1, D_MODEL), lambda i, f: (0, 0)),
            pl.BlockSpec((D_MODEL, tf), lambda i, f: (0, f)),
            pl.BlockSpec((tf, D_MODEL), lambda i, f: (f, 0)),
        ],
        out_specs=pl.BlockSpec((tm, D_MODEL), lambda i, f: (i, 0)),
        out_shape=jax.ShapeDtypeStruct((TOKENS, D_MODEL), F32),
        scratch_shapes=[pltpu.VMEM((tm, D_MODEL), BF16)],
        compiler_params=_params(("parallel", "arbitrary")),
        name="mlp",
    )(h, gain, w1, w2)


def _att_inproj_kernel(h_ref, gain_ref, w_ref, qg_ref, kg_ref, cos_ref, sin_up_ref, sin_dn_ref,
                       o_ref, hn_ref, *, n_q_tiles, n_qk_tiles, heads_per_tile):
    j = pl.program_id(1)

    @pl.when(j == 0)
    def _():
        hn_ref[...] = _rms_norm_bf16(h_ref[...], gain_ref[...])

    acc = jnp.dot(hn_ref[...], w_ref[...], preferred_element_type=F32)

    @pl.when(j >= n_qk_tiles)
    def _():
        o_ref[...] = acc.astype(o_ref.dtype)

    @pl.when(j < n_qk_tiles)
    def _():
        is_q = j < n_q_tiles
        head_gain = jnp.where(is_q, qg_ref[...], kg_ref[...])
        scale = jnp.where(is_q, ATT_HEAD_DIM ** -0.5, 1.0).astype(F32)
        c = cos_ref[...]
        s_up = sin_up_ref[...]
        s_dn = sin_dn_ref[...]
        for hh in range(heads_per_tile):
            cols = slice(hh * ATT_HEAD_DIM, (hh + 1) * ATT_HEAD_DIM)
            x = acc[:, cols]
            ms = jnp.mean(x * x, axis=-1, keepdims=True)
            xn = x * lax.rsqrt(ms + EPS) * head_gain
            rot = (xn * c + pltpu.roll(xn, ROPE_HALF, axis=1) * s_up
                   + pltpu.roll(xn, LANES - ROPE_HALF, axis=1) * s_dn)
            o_ref[:, cols] = (rot * scale).astype(o_ref.dtype)


def _att_inproj(h, gain, w, q_gain, k_gain, cos, sin_up, sin_dn, *, tm=1024, tn=1024):
    n_pos_blocks = SEQ // tm
    group_width = N_ATT_GROUPS * ATT_WIDTH
    kern = functools.partial(
        _att_inproj_kernel, n_q_tiles=group_width // tn, n_qk_tiles=2 * group_width // tn,
        heads_per_tile=tn // ATT_HEAD_DIM)
    pos_spec = pl.BlockSpec((tm, ATT_HEAD_DIM), lambda i, j: (i % n_pos_blocks, 0))
    return pl.pallas_call(
        kern,
        grid=(TOKENS // tm, ATT_IN_WIDTH // tn),
        in_specs=[
            pl.BlockSpec((tm, D_MODEL), lambda i, j: (i, 0)),
            pl.BlockSpec((1, D_MODEL), lambda i, j: (0, 0)),
            pl.BlockSpec((D_MODEL, tn), lambda i, j: (0, j)),
            pl.BlockSpec((1, ATT_HEAD_DIM), lambda i, j: (0, 0)),
            pl.BlockSpec((1, ATT_HEAD_DIM), lambda i, j: (0, 0)),
            pos_spec, pos_spec, pos_spec,
        ],
        out_specs=pl.BlockSpec((tm, tn), lambda i, j: (i, j)),
        out_shape=jax.ShapeDtypeStruct((TOKENS, ATT_IN_WIDTH), BF16),
        scratch_shapes=[pltpu.VMEM((tm, D_MODEL), BF16)],
        compiler_params=_params(("parallel", "arbitrary")),
        name="att_inproj",
    )(h, gain, w, q_gain, k_gain, cos, sin_up, sin_dn)


def _attention_kernel(q_ref, kp_ref, ko_ref, vp_ref, vo_ref, o_ref, lse_ref):
    n = pl.program_id(2)
    qi = lax.broadcasted_iota(jnp.int32, (ATT_BLOCK, ATT_BLOCK), 0)
    kj = lax.broadcasted_iota(jnp.int32, (ATT_BLOCK, ATT_BLOCK), 1)
    prev_valid = (kj >= qi) & (n > 0)
    own_valid = kj <= qi
    lane = lax.broadcasted_iota(jnp.int32, (ATT_BLOCK, LANES), 1)
    lse_tile = jnp.zeros((ATT_BLOCK, LANES), F32)
    contract_last = (((1,), (1,)), ((), ()))
    for h in range(ATT_HEADS):
        cols = slice(h * ATT_HEAD_DIM, (h + 1) * ATT_HEAD_DIM)
        q = q_ref[:, cols]
        s_prev = lax.dot_general(q, kp_ref[:, cols], contract_last, preferred_element_type=F32)
        s_own = lax.dot_general(q, ko_ref[:, cols], contract_last, preferred_element_type=F32)
        s_prev = jnp.where(prev_valid, s_prev, -jnp.inf)
        s_own = jnp.where(own_valid, s_own, -jnp.inf)
        m = jnp.maximum(jnp.max(s_prev, axis=-1, keepdims=True),
                        jnp.max(s_own, axis=-1, keepdims=True))
        p_prev = jnp.exp(s_prev - m)
        p_own = jnp.exp(s_own - m)
        denom = jnp.sum(p_prev, axis=-1, keepdims=True) + jnp.sum(p_own, axis=-1, keepdims=True)
        pv = (jnp.dot(p_prev.astype(BF16), vp_ref[:, cols], preferred_element_type=F32)
              + jnp.dot(p_own.astype(BF16), vo_ref[:, cols], preferred_element_type=F32))
        o_ref[:, cols] = (pv / denom).astype(o_ref.dtype)
        lse_tile = jnp.where(lane == h, m + jnp.log(denom), lse_tile)
    lse_ref[...] = lse_tile


def _attention_group(qkv, group, dilation):
    sub_len = SEQ // dilation
    n_blk = sub_len // ATT_BLOCK
    units = ATT_IN_WIDTH // ATT_WIDTH
    q_unit, k_unit, v_unit = group, N_ATT_GROUPS + group, 2 * N_ATT_GROUPS + group
    view = qkv.reshape(TOKENS // dilation, dilation * ATT_IN_WIDTH)

    def spec(unit, prev):
        def index(b, c, n):
            blk = jnp.maximum(n - 1, 0) if prev else n
            return (b * n_blk + blk, c * units + unit)
        return pl.BlockSpec((ATT_BLOCK, ATT_WIDTH), index)

    out, lse = pl.pallas_call(
        _attention_kernel,
        grid=(BATCH, dilation, n_blk),
        in_specs=[spec(q_unit, False), spec(k_unit, True), spec(k_unit, False),
                  spec(v_unit, True), spec(v_unit, False)],
        out_specs=[
            pl.BlockSpec((ATT_BLOCK, ATT_WIDTH), lambda b, c, n: (b * n_blk + n, c)),
            pl.BlockSpec((ATT_BLOCK, LANES), lambda b, c, n: (b * n_blk + n, c)),
        ],
        out_shape=[
            jax.ShapeDtypeStruct((TOKENS // dilation, dilation * ATT_WIDTH), BF16),
            jax.ShapeDtypeStruct((TOKENS // dilation, dilation * LANES), F32),
        ],
        compiler_params=_params(("parallel", "parallel", "arbitrary")),
        name=f"attention_d{dilation}",
    )(view, view, view, view, view)
    return out.reshape(TOKENS, ATT_WIDTH), lse.reshape(TOKENS, LANES)


def _att_outproj_kernel(o0_ref, o1_ref, o2_ref, l0_ref, l1_ref, l2_ref, w_ref, res_ref, out_ref):
    l0, l1, l2 = l0_ref[...], l1_ref[...], l2_ref[...]
    top = jnp.maximum(jnp.maximum(l0, l1), l2)
    e0, e1, e2 = jnp.exp(l0 - top), jnp.exp(l1 - top), jnp.exp(l2 - top)
    total = e0 + e1 + e2
    w0, w1, w2 = e0 / total, e1 / total, e2 / total
    parts = []
    for h in range(ATT_HEADS):
        cols = slice(h * ATT_HEAD_DIM, (h + 1) * ATT_HEAD_DIM)
        merged = (w0[:, h:h + 1] * o0_ref[:, cols].astype(F32)
                  + w1[:, h:h + 1] * o1_ref[:, cols].astype(F32)
                  + w2[:, h:h + 1] * o2_ref[:, cols].astype(F32))
        parts.append(merged.astype(BF16))
    merged = jnp.concatenate(parts, axis=1)
    out_ref[...] = res_ref[...] + jnp.dot(merged, w_ref[...], preferred_element_type=F32)


def _att_outproj(outs, lses, w, res, *, tm=256):
    o_spec = pl.BlockSpec((tm, ATT_WIDTH), lambda i: (i, 0))
    l_spec = pl.BlockSpec((tm, LANES), lambda i: (i, 0))
    return pl.pallas_call(
        _att_outproj_kernel,
        grid=(TOKENS // tm,),
        in_specs=[o_spec, o_spec, o_spec, l_spec, l_spec, l_spec,
                  pl.BlockSpec((ATT_WIDTH, D_MODEL), lambda i: (0, 0)),
                  pl.BlockSpec((tm, D_MODEL), lambda i: (i, 0))],
        out_specs=pl.BlockSpec((tm, D_MODEL), lambda i: (i, 0)),
        out_shape=jax.ShapeDtypeStruct((TOKENS, D_MODEL), F32),
        compiler_params=_params(("parallel",)),
        name="att_outproj",
    )(*outs, *lses, w, res)


def _retention_tables():
    pos = jnp.arange(SEQ, dtype=F32)
    inv_freq = 1.0 / (RET_ROT_BASE ** jnp.linspace(0.0, 1.0, RET_DK // 2, dtype=F32))
    ang = pos[:, None] * inv_freq[None, :]
    log_gamma = jnp.log(1.0 - 2.0 ** (-5.0 - jnp.arange(RET_HEADS, dtype=F32)))
    idx = jnp.arange(RET_CHUNK, dtype=F32)
    diff = idx[:, None] - idx[None, :]
    inner_decay = jnp.where(diff >= 0,
                            jnp.exp(log_gamma[:, None, None] * jnp.maximum(diff, 0.0)), 0.0)
    query_decay = jnp.exp(log_gamma[:, None] * (idx[None, :] + 1.0))[:, :, None]
    key_decay = jnp.exp(log_gamma[:, None] * (RET_CHUNK - 1.0 - idx[None, :]))[:, :, None]
    chunk_decay = jnp.broadcast_to(jnp.exp(log_gamma * RET_CHUNK)[:, None, None],
                                   (RET_HEADS, 1, RET_DV))
    return jnp.cos(ang), jnp.sin(ang), inner_decay, query_decay, key_decay, chunk_decay


def _attention_tables():
    pos = jnp.arange(SEQ, dtype=F32)
    inv_freq = ROPE_THETA ** (-jnp.arange(0, ROPE_DIM, 2, dtype=F32) / ROPE_DIM)
    ang = pos[:, None] * inv_freq[None, :]
    cos, sin = jnp.cos(ang), jnp.sin(ang)
    zeros = jnp.zeros((SEQ, ATT_HEAD_DIM - ROPE_DIM), F32)
    zero_half = jnp.zeros((SEQ, ROPE_HALF), F32)
    cos_full = jnp.concatenate([cos, cos, jnp.ones_like(zeros)], axis=1)
    sin_up = jnp.concatenate([zero_half, sin, zeros], axis=1)
    sin_dn = jnp.concatenate([-sin, zero_half, zeros], axis=1)
    return cos_full, sin_up, sin_dn


def kernel(x, norm_mix_gain, norm_mlp_gain, ret_w_in, ret_w_out, att_w_in, att_q_gain,
           att_k_gain, att_w_out, mlp_w_in, mlp_w_out):
    h = x.reshape(TOKENS, D_MODEL)

    cos, sin, inner_decay, query_decay, key_decay, chunk_decay = _retention_tables()
    proj = _ret_inproj(h, norm_mix_gain[0:1], ret_w_in[0].astype(BF16), cos, sin)
    y = _retention(proj, inner_decay, query_decay, key_decay, chunk_decay)
    h = _outproj(y, ret_w_out[0].astype(BF16), h)
    h = _mlp(h, norm_mlp_gain[0:1], mlp_w_in[0].astype(BF16), mlp_w_out[0].astype(BF16))

    cos_full, sin_up, sin_dn = _attention_tables()
    qkv = _att_inproj(h, norm_mix_gain[1:2], att_w_in[0].astype(BF16), att_q_gain[0:1],
                      att_k_gain[0:1], cos_full, sin_up, sin_dn)
    outs, lses = [], []
    for g, (_, dilation) in enumerate(ATT_GROUPS):
        o, l = _attention_group(qkv, g, dilation)
        outs.append(o)
        lses.append(l)
    h = _att_outproj(outs, lses, att_w_out[0].astype(BF16), h)
    h = _mlp(h, norm_mlp_gain[1:2], mlp_w_in[1].astype(BF16), mlp_w_out[1].astype(BF16))
    return h.reshape(BATCH, SEQ, D_MODEL)
```

```python
import functools

import jax
import jax.numpy as jnp
from jax import lax
from jax.experimental import pallas as pl
from jax.experimental.pallas import tpu as pltpu

F32 = jnp.float32
BF16 = jnp.bfloat16

D_MODEL = 2048
BATCH = 2
SEQ = 8192
TOKENS = BATCH * SEQ
EPS = 1e-6

RET_HEADS = 8
RET_DK = 256
RET_DV = 512
RET_QK_WIDTH = RET_HEADS * RET_DK
RET_V_WIDTH = RET_HEADS * RET_DV
RET_IN_WIDTH = 2 * RET_QK_WIDTH + 2 * RET_V_WIDTH
RET_CHUNK = 128
RET_ROT_BASE = 10000.0

ATT_GROUPS = ((128, 1), (512, 4), (2048, 16))
N_ATT_GROUPS = 3
ATT_HEAD_DIM = 128
ATT_HEADS = 16
ATT_WIDTH = ATT_HEADS * ATT_HEAD_DIM
ATT_IN_WIDTH = 3 * N_ATT_GROUPS * ATT_WIDTH
ATT_BLOCK = 128
ROPE_DIM = 32
ROPE_HALF = ROPE_DIM // 2
ROPE_THETA = 500000.0

D_FF = 4 * D_MODEL

LANES = 128
VMEM_LIMIT = 56 * 1024 * 1024


def _params(semantics):
    return pltpu.CompilerParams(dimension_semantics=semantics, vmem_limit_bytes=VMEM_LIMIT)


def _rms_norm_bf16(x, gain):
    ms = jnp.mean(x * x, axis=-1, keepdims=True)
    return (x * lax.rsqrt(ms + EPS) * gain).astype(BF16)


def _ret_inproj_kernel(h_ref, gain_ref, w_ref, cos_ref, sin_ref, o_ref, hn_ref, *,
                       n_q_tiles, n_rot_tiles, heads_per_tile):
    j = pl.program_id(1)

    @pl.when(j == 0)
    def _():
        hn_ref[...] = _rms_norm_bf16(h_ref[...], gain_ref[...])

    def slab(hh):
        cols = slice(hh * RET_DK, (hh + 1) * RET_DK)
        return jnp.dot(hn_ref[...], w_ref[:, cols], preferred_element_type=F32)

    @pl.when(j >= n_rot_tiles)
    def _():
        for hh in range(heads_per_tile):
            o_ref[:, hh * RET_DK:(hh + 1) * RET_DK] = slab(hh).astype(o_ref.dtype)

    @pl.when(j < n_rot_tiles)
    def _():
        scale = jnp.where(j >= n_q_tiles, RET_DK ** -0.5, 1.0).astype(F32)
        c = cos_ref[...] * scale
        s = sin_ref[...] * scale
        half = RET_DK // 2
        for hh in range(heads_per_tile):
            acc = slab(hh)
            lo = hh * RET_DK
            x1 = acc[:, :half]
            x2 = acc[:, half:]
            o_ref[:, lo:lo + half] = (x1 * c - x2 * s).astype(o_ref.dtype)
            o_ref[:, lo + half:lo + RET_DK] = (x2 * c + x1 * s).astype(o_ref.dtype)


def _ret_inproj(h, gain, w, cos, sin, *, tm=1024, tn=1024):
    n_pos_blocks = SEQ // tm
    kern = functools.partial(
        _ret_inproj_kernel, n_q_tiles=RET_QK_WIDTH // tn, n_rot_tiles=2 * RET_QK_WIDTH // tn,
        heads_per_tile=tn // RET_DK)
    return pl.pallas_call(
        kern,
        grid=(TOKENS // tm, RET_IN_WIDTH // tn),
        in_specs=[
            pl.BlockSpec((tm, D_MODEL), lambda i, j: (i, 0)),
            pl.BlockSpec((1, D_MODEL), lambda i, j: (0, 0)),
            pl.BlockSpec((D_MODEL, tn), lambda i, j: (0, j)),
            pl.BlockSpec((tm, RET_DK // 2), lambda i, j: (i % n_pos_blocks, 0)),
            pl.BlockSpec((tm, RET_DK // 2), lambda i, j: (i % n_pos_blocks, 0)),
        ],
        out_specs=pl.BlockSpec((tm, tn), lambda i, j: (i, j)),
        out_shape=jax.ShapeDtypeStruct((TOKENS, RET_IN_WIDTH), BF16),
        scratch_shapes=[pltpu.VMEM((tm, D_MODEL), BF16)],
        compiler_params=_params(("parallel", "arbitrary")),
        name="ret_inproj",
    )(h, gain, w, cos, sin)


RET_BLOCK = 256


def _retention_kernel(q_ref, k_ref, v_ref, g_ref, dec_ref, qd_ref, kd_ref, cd_ref, o_ref,
                      state_ref, acc_ref, upd_ref, *, n_chunks):
    @pl.when(pl.program_id(2) == 0)
    def _():
        state_ref[...] = jnp.zeros_like(state_ref)

    dec = dec_ref[...]
    qd = qd_ref[...]
    kd = kd_ref[...]
    cd = cd_ref[...]
    chunk = lambda c: slice(c * RET_BLOCK, (c + 1) * RET_BLOCK)

    for c in range(n_chunks):
        qc, kc, vc = q_ref[chunk(c), :], k_ref[chunk(c), :], v_ref[chunk(c), :]
        scores = lax.dot_general(qc, kc, (((1,), (1,)), ((), ())),
                                 preferred_element_type=F32) * dec
        acc_ref[chunk(c), :] = jnp.dot(scores.astype(BF16), vc, preferred_element_type=F32)
        k_dec = (kc.astype(F32) * kd).astype(BF16)
        upd_ref[c] = lax.dot_general(k_dec, vc, (((0,), (0,)), ((), ())),
                                     preferred_element_type=F32)

    for c in range(n_chunks):
        state = state_ref[...]
        cross = jnp.dot(q_ref[chunk(c), :], state.astype(BF16), preferred_element_type=F32) * qd
        state_ref[...] = state * cd + upd_ref[c]
        out = acc_ref[chunk(c), :] + cross
        ms = jnp.mean(out * out, axis=-1, keepdims=True)
        gate = g_ref[chunk(c), :].astype(F32)
        y = out * lax.rsqrt(ms + EPS) * (gate * jax.nn.sigmoid(gate))
        o_ref[chunk(c), :] = y.astype(o_ref.dtype)


def _retention(proj, dec, qd, kd, cd, *, rows=1024):
    n_row_blocks = SEQ // rows
    n_chunks = rows // RET_BLOCK
    k_off = RET_QK_WIDTH // RET_DK
    v_off = 2 * RET_QK_WIDTH // RET_DV
    g_off = v_off + RET_V_WIDTH // RET_DV
    row = lambda b, h, r: b * n_row_blocks + r
    kern = functools.partial(_retention_kernel, n_chunks=n_chunks)
    return pl.pallas_call(
        kern,
        grid=(BATCH, RET_HEADS, n_row_blocks),
        in_specs=[
            pl.BlockSpec((rows, RET_DK), lambda b, h, r: (row(b, h, r), h)),
            pl.BlockSpec((rows, RET_DK), lambda b, h, r: (row(b, h, r), k_off + h)),
            pl.BlockSpec((rows, RET_DV), lambda b, h, r: (row(b, h, r), v_off + h)),
            pl.BlockSpec((rows, RET_DV), lambda b, h, r: (row(b, h, r), g_off + h)),
            pl.BlockSpec((None, RET_BLOCK, RET_BLOCK), lambda b, h, r: (h, 0, 0)),
            pl.BlockSpec((None, RET_BLOCK, 1), lambda b, h, r: (h, 0, 0)),
            pl.BlockSpec((None, RET_BLOCK, 1), lambda b, h, r: (h, 0, 0)),
            pl.BlockSpec((None, 1, RET_DV), lambda b, h, r: (h, 0, 0)),
        ],
        out_specs=pl.BlockSpec((rows, RET_DV), lambda b, h, r: (row(b, h, r), h)),
        out_shape=jax.ShapeDtypeStruct((TOKENS, RET_V_WIDTH), BF16),
        scratch_shapes=[pltpu.VMEM((RET_DK, RET_DV), F32),
                        pltpu.VMEM((rows, RET_DV), F32),
                        pltpu.VMEM((n_chunks, RET_DK, RET_DV), F32)],
        compiler_params=_params(("parallel", "parallel", "arbitrary")),
        name="retention",
    )(proj, proj, proj, proj, dec, qd, kd, cd)


def _outproj_kernel(y_ref, w_ref, res_ref, o_ref):
    o_ref[...] = res_ref[...] + jnp.dot(y_ref[...], w_ref[...], preferred_element_type=F32)


def _outproj(y, w, res, *, tm=512, tn=1024):
    k = y.shape[1]
    return pl.pallas_call(
        _outproj_kernel,
        grid=(D_MODEL // tn, TOKENS // tm),
        in_specs=[
            pl.BlockSpec((tm, k), lambda j, i: (i, 0)),
            pl.BlockSpec((k, tn), lambda j, i: (0, j)),
            pl.BlockSpec((tm, tn), lambda j, i: (i, j)),
        ],
        out_specs=pl.BlockSpec((tm, tn), lambda j, i: (i, j)),
        out_shape=jax.ShapeDtypeStruct((TOKENS, D_MODEL), F32),
        compiler_params=_params(("parallel", "parallel")),
        name="outproj",
    )(y, w, res)


def _mlp_kernel(h_ref, gain_ref, w1_ref, w2_ref, o_ref, hn_ref):
    @pl.when(pl.program_id(1) == 0)
    def _():
        h = h_ref[...]
        hn_ref[...] = _rms_norm_bf16(h, gain_ref[...])
        o_ref[...] = h

    a = jnp.dot(hn_ref[...], w1_ref[...], preferred_element_type=F32)
    act = jnp.square(jnp.maximum(a, 0.0)).astype(BF16)
    o_ref[...] += jnp.dot(act, w2_ref[...], preferred_element_type=F32)


def _mlp(h, gain, w1, w2, *, tm=512, tf=1024):
    return pl.pallas_call(
        _mlp_kernel,
        grid=(TOKENS // tm, D_FF // tf),
        in_specs=[
            pl.BlockSpec((tm, D_MODEL), lambda i, f: (i, 0)),
            pl.BlockSpec((1, D_MODEL), lambda i, f: (0, 0)),
            pl.BlockSpec((D_MODEL, tf), lambda i, f: (0, f)),
            pl.BlockSpec((tf, D_MODEL), lambda i, f: (f, 0)),
        ],
        out_specs=pl.BlockSpec((tm, D_MODEL), lambda i, f: (i, 0)),
        out_shape=jax.ShapeDtypeStruct((TOKENS, D_MODEL), F32),
        scratch_shapes=[pltpu.VMEM((tm, D_MODEL), BF16)],
        compiler_params=_params(("parallel", "arbitrary")),
        name="mlp",
    )(h, gain, w1, w2)


ATT_TILE = 256
ATT_SUB = 256


def _residue_major_matrix(dilation):
    r = jnp.arange(ATT_TILE)
    n = ATT_TILE // dilation
    src = (r % n) * dilation + r // n
    return (src[:, None] == r[None, :]).astype(BF16)


def _head_aux_matrix():
    i = jnp.arange(2 * ATT_HEAD_DIM)
    row, col = i[:, None], i[None, :]
    ones = (row < ATT_HEAD_DIM) & (col < ATT_HEAD_DIM)
    r, c = row - ATT_HEAD_DIM, col - ATT_HEAD_DIM
    minus = (c >= 0) & (c < ROPE_HALF) & (r == c + ROPE_HALF)
    plus = (r >= 0) & (r < ROPE_HALF) & (c == r + ROPE_HALF)
    return (ones.astype(F32) + plus.astype(F32) - minus.astype(F32)).astype(BF16)


def _att_inproj_kernel(h_ref, gain_ref, w_ref, qg_ref, kg_ref, cos_ref, sin_ref, aux_ref,
                       p1_ref, p2_ref, o_ref, hn_ref, *, tiles_per_unit, n_q_tiles, n_qk_tiles):
    j = pl.program_id(1)
    tm, tn = o_ref.shape

    @pl.when(j == 0)
    def _():
        hn_ref[0] = _rms_norm_bf16(h_ref[...], gain_ref[...])
        for g, p_ref in ((1, p1_ref), (2, p2_ref)):
            for t in range(tm // ATT_TILE):
                rows = slice(t * ATT_TILE, (t + 1) * ATT_TILE)
                hn_ref[g, rows, :] = jnp.dot(p_ref[...], hn_ref[0, rows, :],
                                             preferred_element_type=F32).astype(BF16)

    g = (j // tiles_per_unit) % N_ATT_GROUPS

    n_sub = tn // ATT_SUB

    def slab(k):
        return jnp.dot(hn_ref[g], w_ref[:, k * ATT_SUB:(k + 1) * ATT_SUB],
                       preferred_element_type=F32)

    @pl.when(j >= n_qk_tiles)
    def _():
        for k in range(n_sub):
            o_ref[:, k * ATT_SUB:(k + 1) * ATT_SUB] = slab(k).astype(o_ref.dtype)

    @pl.when(j < n_qk_tiles)
    def _():
        is_q = j < n_q_tiles
        head_gain = jnp.where(is_q, qg_ref[...] * (ATT_HEAD_DIM ** -0.5), kg_ref[...])
        c = cos_ref[...]
        s = sin_ref[...]

        def epilogue(k, acc):
            for hh in range(ATT_SUB // ATT_HEAD_DIM):
                x = acc[:, hh * ATT_HEAD_DIM:(hh + 1) * ATT_HEAD_DIM]
                xg = x * head_gain
                lhs = jnp.concatenate([(x * x).astype(BF16), xg.astype(BF16)], axis=1)
                aux = jnp.dot(lhs, aux_ref[...], preferred_element_type=F32)
                inv = lax.rsqrt(aux[:, :ATT_HEAD_DIM] * (1.0 / ATT_HEAD_DIM) + EPS)
                rot = (xg * c + aux[:, ATT_HEAD_DIM:] * s) * inv
                lo = k * ATT_SUB + hh * ATT_HEAD_DIM
                o_ref[:, lo:lo + ATT_HEAD_DIM] = rot.astype(o_ref.dtype)

        acc = slab(0)
        for k in range(n_sub):
            nxt = slab(k + 1) if k + 1 < n_sub else None
            epilogue(k, acc)
            acc = nxt


def _att_inproj(h, gain, w, q_gain, k_gain, cos, sin, *, tm=1024, tn=1024):
    n_pos_blocks = SEQ // tm
    group_width = N_ATT_GROUPS * ATT_WIDTH
    tiles_per_unit = ATT_WIDTH // tn
    kern = functools.partial(
        _att_inproj_kernel, tiles_per_unit=tiles_per_unit, n_q_tiles=group_width // tn,
        n_qk_tiles=2 * group_width // tn)
    pos_spec = pl.BlockSpec(
        (None, tm, ATT_HEAD_DIM),
        lambda i, j: ((j // tiles_per_unit) % N_ATT_GROUPS, i % n_pos_blocks, 0))
    return pl.pallas_call(
        kern,
        grid=(TOKENS // tm, ATT_IN_WIDTH // tn),
        in_specs=[
            pl.BlockSpec((tm, D_MODEL), lambda i, j: (i, 0)),
            pl.BlockSpec((1, D_MODEL), lambda i, j: (0, 0)),
            pl.BlockSpec((D_MODEL, tn), lambda i, j: (0, j)),
            pl.BlockSpec((1, ATT_HEAD_DIM), lambda i, j: (0, 0)),
            pl.BlockSpec((1, ATT_HEAD_DIM), lambda i, j: (0, 0)),
            pos_spec, pos_spec,
            pl.BlockSpec((2 * ATT_HEAD_DIM, 2 * ATT_HEAD_DIM), lambda i, j: (0, 0)),
            pl.BlockSpec((ATT_TILE, ATT_TILE), lambda i, j: (0, 0)),
            pl.BlockSpec((ATT_TILE, ATT_TILE), lambda i, j: (0, 0)),
        ],
        out_specs=pl.BlockSpec((tm, tn), lambda i, j: (i, j)),
        out_shape=jax.ShapeDtypeStruct((TOKENS, ATT_IN_WIDTH), BF16),
        scratch_shapes=[pltpu.VMEM((N_ATT_GROUPS, tm, D_MODEL), BF16)],
        compiler_params=_params(("parallel", "arbitrary")),
        name="att_inproj",
    )(h, gain, w, q_gain, k_gain, cos, sin, _head_aux_matrix(),
      _residue_major_matrix(ATT_GROUPS[1][1]), _residue_major_matrix(ATT_GROUPS[2][1]))


def _attention_kernel(q_ref, ko_ref, vo_ref, o_ref, lse_ref, kp_ref, vp_ref, s_ref):
    n = pl.program_id(2)

    @pl.when(n == 0)
    def _():
        kp_ref[...] = jnp.zeros_like(kp_ref)
        vp_ref[...] = jnp.zeros_like(vp_ref)

    def tile(ref, cols):
        return ref[..., cols].reshape(ATT_BLOCK, ATT_HEAD_DIM)

    qi = lax.broadcasted_iota(jnp.int32, (ATT_BLOCK, 2 * ATT_BLOCK), 0)
    kj = lax.broadcasted_iota(jnp.int32, (ATT_BLOCK, 2 * ATT_BLOCK), 1)
    valid = (kj >= qi) & (kj <= qi + ATT_BLOCK) & ((n > 0) | (kj >= ATT_BLOCK))
    lane = lax.broadcasted_iota(jnp.int32, (ATT_BLOCK, LANES), 1)
    contract_last = (((1,), (1,)), ((), ()))
    for h in range(ATT_HEADS):
        cols = slice(h * ATT_HEAD_DIM, (h + 1) * ATT_HEAD_DIM)
        q = tile(q_ref, cols)
        s_ref[h, :, :ATT_BLOCK] = lax.dot_general(q, tile(kp_ref, cols), contract_last,
                                                  preferred_element_type=F32)
        s_ref[h, :, ATT_BLOCK:] = lax.dot_general(q, tile(ko_ref, cols), contract_last,
                                                  preferred_element_type=F32)
    lse_tile = jnp.zeros((ATT_BLOCK, LANES), F32)
    for h in range(ATT_HEADS):
        cols = slice(h * ATT_HEAD_DIM, (h + 1) * ATT_HEAD_DIM)
        s = jnp.where(valid, s_ref[h], -jnp.inf)
        m = jnp.max(s, axis=-1, keepdims=True)
        p = jnp.exp(s - m)
        denom = jnp.sum(p, axis=-1, keepdims=True)
        pb = p.astype(BF16)
        pv = (jnp.dot(pb[:, :ATT_BLOCK], tile(vp_ref, cols), preferred_element_type=F32)
              + jnp.dot(pb[:, ATT_BLOCK:], tile(vo_ref, cols), preferred_element_type=F32))
        out = (pv / denom).astype(o_ref.dtype)
        o_ref[..., cols] = out.reshape(o_ref.shape[:-1] + (ATT_HEAD_DIM,))
        lse_tile = jnp.where(lane == h, m + jnp.log(denom), lse_tile)
    lse_ref[...] = lse_tile.reshape(lse_ref.shape)
    kp_ref[...] = ko_ref[...]
    vp_ref[...] = vo_ref[...]


def _attention_group(qkv, group, dilation):
    n_sub = ATT_TILE // dilation
    n_blk = SEQ // dilation // ATT_BLOCK
    n_tiles = TOKENS // ATT_TILE
    tiles_per_batch = SEQ // ATT_TILE
    q_unit, k_unit, v_unit = group, N_ATT_GROUPS + group, 2 * N_ATT_GROUPS + group
    view = qkv.reshape(n_tiles, dilation, n_sub, ATT_IN_WIDTH)

    if n_sub >= ATT_BLOCK:
        per_tile = n_sub // ATT_BLOCK
        lead = (None, None, ATT_BLOCK)
        scratch_lead = (ATT_BLOCK,)
        index = lambda b, c, n: (b * tiles_per_batch + n // per_tile, c, n % per_tile)
    else:
        span = ATT_BLOCK // n_sub
        lead = (span, None, n_sub)
        scratch_lead = (span, n_sub)
        index = lambda b, c, n: (b * (tiles_per_batch // span) + n, c, 0)

    def spec(width, unit):
        return pl.BlockSpec(lead + (width,), lambda b, c, n: index(b, c, n) + (unit,))

    return pl.pallas_call(
        _attention_kernel,
        grid=(BATCH, dilation, n_blk),
        in_specs=[spec(ATT_WIDTH, q_unit), spec(ATT_WIDTH, k_unit), spec(ATT_WIDTH, v_unit)],
        out_specs=[spec(ATT_WIDTH, 0), spec(LANES, 0)],
        out_shape=[
            jax.ShapeDtypeStruct((n_tiles, dilation, n_sub, ATT_WIDTH), BF16),
            jax.ShapeDtypeStruct((n_tiles, dilation, n_sub, LANES), F32),
        ],
        scratch_shapes=[
            pltpu.VMEM(scratch_lead + (ATT_WIDTH,), BF16),
            pltpu.VMEM(scratch_lead + (ATT_WIDTH,), BF16),
            pltpu.VMEM((ATT_HEADS, ATT_BLOCK, 2 * ATT_BLOCK), F32),
        ],
        compiler_params=_params(("parallel", "parallel", "arbitrary")),
        name=f"attention_d{dilation}",
    )(view, view, view)


OUT_TM = 256


def _att_outproj_kernel(o0_ref, o1_ref, o2_ref, l0_ref, l1_ref, l2_ref, p1_ref, p2_ref, w_ref,
                        res_ref, out_ref, lse_ref):
    for k, l_ref in enumerate((l1_ref, l2_ref)):
        d, n = l_ref.shape[0], l_ref.shape[1]
        for c in range(d):
            lse_ref[k, pl.ds(c, n, stride=d), :] = l_ref[c]
    l0, l1, l2 = l0_ref[0], lse_ref[0], lse_ref[1]
    top = jnp.maximum(jnp.maximum(l0, l1), l2)
    e0, e1, e2 = jnp.exp(l0 - top), jnp.exp(l1 - top), jnp.exp(l2 - top)
    total = e0 + e1 + e2
    w0, w1, w2 = e0 / total, e1 / total, e2 / total
    o1 = jnp.dot(p1_ref[...], o1_ref[...].reshape(OUT_TM, ATT_WIDTH), preferred_element_type=F32)
    o2 = jnp.dot(p2_ref[...], o2_ref[...].reshape(OUT_TM, ATT_WIDTH), preferred_element_type=F32)
    parts = []
    for h in range(ATT_HEADS):
        cols = slice(h * ATT_HEAD_DIM, (h + 1) * ATT_HEAD_DIM)
        merged = (w0[:, h:h + 1] * o0_ref[0, :, cols].astype(F32)
                  + w1[:, h:h + 1] * o1[:, cols]
                  + w2[:, h:h + 1] * o2[:, cols])
        parts.append(merged.astype(BF16))
    merged = jnp.concatenate(parts, axis=1)
    out_ref[...] = res_ref[...] + jnp.dot(merged, w_ref[...], preferred_element_type=F32)


def _token_order_matrix(dilation):
    t = jnp.arange(OUT_TM)
    src = (t % dilation) * (OUT_TM // dilation) + t // dilation
    return (src[:, None] == t[None, :]).astype(BF16)


def _att_outproj(outs, lses, w, res):
    tm = OUT_TM
    per_tile = ATT_TILE // tm

    def spec(dilation, width):
        return pl.BlockSpec((None, dilation, tm // dilation, width),
                            lambda i: (i // per_tile, 0, i % per_tile, 0))

    dils = [d for _, d in ATT_GROUPS]
    const = lambda shape: pl.BlockSpec(shape, lambda i: (0, 0))
    return pl.pallas_call(
        _att_outproj_kernel,
        grid=(TOKENS // tm,),
        in_specs=[spec(d, ATT_WIDTH) for d in dils] + [spec(d, LANES) for d in dils] + [
            const((tm, tm)), const((tm, tm)), const((ATT_WIDTH, D_MODEL)),
            pl.BlockSpec((tm, D_MODEL), lambda i: (i, 0))],
        out_specs=pl.BlockSpec((tm, D_MODEL), lambda i: (i, 0)),
        out_shape=jax.ShapeDtypeStruct((TOKENS, D_MODEL), F32),
        scratch_shapes=[pltpu.VMEM((N_ATT_GROUPS - 1, tm, LANES), F32)],
        compiler_params=_params(("parallel",)),
        name="att_outproj",
    )(*outs, *lses, _token_order_matrix(dils[1]), _token_order_matrix(dils[2]), w, res)


def _retention_tables():
    pos = jnp.arange(SEQ, dtype=F32)
    inv_freq = 1.0 / (RET_ROT_BASE ** jnp.linspace(0.0, 1.0, RET_DK // 2, dtype=F32))
    ang = pos[:, None] * inv_freq[None, :]
    log_gamma = jnp.log(1.0 - 2.0 ** (-5.0 - jnp.arange(RET_HEADS, dtype=F32)))
    idx = jnp.arange(RET_BLOCK, dtype=F32)
    diff = idx[:, None] - idx[None, :]
    inner_decay = jnp.where(diff >= 0,
                            jnp.exp(log_gamma[:, None, None] * jnp.maximum(diff, 0.0)), 0.0)
    query_decay = jnp.exp(log_gamma[:, None] * (idx[None, :] + 1.0))[:, :, None]
    key_decay = jnp.exp(log_gamma[:, None] * (RET_BLOCK - 1.0 - idx[None, :]))[:, :, None]
    chunk_decay = jnp.broadcast_to(jnp.exp(log_gamma * RET_BLOCK)[:, None, None],
                                   (RET_HEADS, 1, RET_DV))
    return jnp.cos(ang), jnp.sin(ang), inner_decay, query_decay, key_decay, chunk_decay


def _attention_tables():
    pos = jnp.arange(SEQ, dtype=F32)
    inv_freq = ROPE_THETA ** (-jnp.arange(0, ROPE_DIM, 2, dtype=F32) / ROPE_DIM)
    ang = pos[:, None] * inv_freq[None, :]
    cos, sin = jnp.cos(ang), jnp.sin(ang)
    zeros = jnp.zeros((SEQ, ATT_HEAD_DIM - ROPE_DIM), F32)
    cos_full = jnp.concatenate([cos, cos, jnp.ones_like(zeros)], axis=1)
    sin_full = jnp.concatenate([sin, sin, zeros], axis=1)

    def per_group(table):
        copies = []
        for _, d in ATT_GROUPS:
            t = table.reshape(SEQ // ATT_TILE, ATT_TILE // d, d, ATT_HEAD_DIM)
            copies.append(t.swapaxes(1, 2).reshape(SEQ, ATT_HEAD_DIM))
        return jnp.stack(copies)

    return per_group(cos_full), per_group(sin_full)


def kernel(x, norm_mix_gain, norm_mlp_gain, ret_w_in, ret_w_out, att_w_in, att_q_gain,
           att_k_gain, att_w_out, mlp_w_in, mlp_w_out):
    h = x.reshape(TOKENS, D_MODEL)

    cos, sin, inner_decay, query_decay, key_decay, chunk_decay = _retention_tables()
    proj = _ret_inproj(h, norm_mix_gain[0:1], ret_w_in[0].astype(BF16), cos, sin)
    y = _retention(proj, inner_decay, query_decay, key_decay, chunk_decay)
    h = _outproj(y, ret_w_out[0].astype(BF16), h)
    h = _mlp(h, norm_mlp_gain[0:1], mlp_w_in[0].astype(BF16), mlp_w_out[0].astype(BF16))

    cos_full, sin_full = _attention_tables()
    qkv = _att_inproj(h, norm_mix_gain[1:2], att_w_in[0].astype(BF16), att_q_gain[0:1],
                      att_k_gain[0:1], cos_full, sin_full)
    outs, lses = [], []
    for g, (_, dilation) in enumerate(ATT_GROUPS):
        o, l = _attention_group(qkv, g, dilation)
        outs.append(o)
        lses.append(l)
    h = _att_outproj(outs, lses, att_w_out[0].astype(BF16), h)
    h = _mlp(h, norm_mlp_gain[1:2], mlp_w_in[1].astype(BF16), mlp_w_out[1].astype(BF16))
    return h.reshape(BATCH, SEQ, D_MODEL)
```

```python
import functools

import jax
import jax.numpy as jnp
from jax import lax
from jax.experimental import pallas as pl
from jax.experimental.pallas import tpu as pltpu

F32 = jnp.float32
BF16 = jnp.bfloat16

D_MODEL = 2048
BATCH = 2
SEQ = 8192
TOKENS = BATCH * SEQ
EPS = 1e-6

RET_HEADS = 8
RET_DK = 256
RET_DV = 512
RET_QK_WIDTH = RET_HEADS * RET_DK
RET_V_WIDTH = RET_HEADS * RET_DV
RET_IN_WIDTH = 2 * RET_QK_WIDTH + 2 * RET_V_WIDTH
RET_CHUNK = 128
RET_ROT_BASE = 10000.0

ATT_GROUPS = ((128, 1), (512, 4), (2048, 16))
N_ATT_GROUPS = 3
ATT_HEAD_DIM = 128
ATT_HEADS = 16
ATT_WIDTH = ATT_HEADS * ATT_HEAD_DIM
ATT_IN_WIDTH = 3 * N_ATT_GROUPS * ATT_WIDTH
ATT_BLOCK = 128
ROPE_DIM = 32
ROPE_HALF = ROPE_DIM // 2
ROPE_THETA = 500000.0

D_FF = 4 * D_MODEL

LANES = 128
VMEM_LIMIT = 56 * 1024 * 1024


def _params(semantics):
    return pltpu.CompilerParams(dimension_semantics=semantics, vmem_limit_bytes=VMEM_LIMIT)


def _rms_norm_bf16(x, gain):
    ms = jnp.mean(x * x, axis=-1, keepdims=True)
    return (x * lax.rsqrt(ms + EPS) * gain).astype(BF16)


def _ret_inproj_kernel(h_ref, gain_ref, w_ref, cos_ref, sin_ref, o_ref, hn_ref, *,
                       n_q_tiles, n_rot_tiles, n_plain_tiles, heads_per_tile):
    j = pl.program_id(1)

    @pl.when(j == 0)
    def _():
        hn_ref[...] = _rms_norm_bf16(h_ref[...], gain_ref[...])

    def slab(hh):
        cols = slice(hh * RET_DK, (hh + 1) * RET_DK)
        return jnp.dot(hn_ref[...], w_ref[:, cols], preferred_element_type=F32)

    @pl.when((j >= n_rot_tiles) & (j < n_plain_tiles))
    def _():
        for hh in range(heads_per_tile):
            o_ref[:, hh * RET_DK:(hh + 1) * RET_DK] = slab(hh).astype(o_ref.dtype)

    @pl.when(j >= n_plain_tiles)
    def _():
        for hh in range(heads_per_tile):
            gate = slab(hh)
            o_ref[:, hh * RET_DK:(hh + 1) * RET_DK] = (
                gate * jax.nn.sigmoid(gate)).astype(o_ref.dtype)

    @pl.when(j < n_rot_tiles)
    def _():
        scale = jnp.where(j >= n_q_tiles, RET_DK ** -0.5, 1.0).astype(F32)
        c = cos_ref[...] * scale
        s = sin_ref[...] * scale
        half = RET_DK // 2
        for hh in range(heads_per_tile):
            acc = slab(hh)
            lo = hh * RET_DK
            x1 = acc[:, :half]
            x2 = acc[:, half:]
            o_ref[:, lo:lo + half] = (x1 * c - x2 * s).astype(o_ref.dtype)
            o_ref[:, lo + half:lo + RET_DK] = (x2 * c + x1 * s).astype(o_ref.dtype)


def _ret_inproj(h, gain, w, cos, sin, *, tm=1024, tn=1024):
    n_pos_blocks = SEQ // tm
    kern = functools.partial(
        _ret_inproj_kernel, n_q_tiles=RET_QK_WIDTH // tn, n_rot_tiles=2 * RET_QK_WIDTH // tn,
        n_plain_tiles=(2 * RET_QK_WIDTH + RET_V_WIDTH) // tn, heads_per_tile=tn // RET_DK)
    return pl.pallas_call(
        kern,
        grid=(TOKENS // tm, RET_IN_WIDTH // tn),
        in_specs=[
            pl.BlockSpec((tm, D_MODEL), lambda i, j: (i, 0)),
            pl.BlockSpec((1, D_MODEL), lambda i, j: (0, 0)),
            pl.BlockSpec((None, D_MODEL, tn), lambda i, j: (0, 0, j)),
            pl.BlockSpec((tm, RET_DK // 2), lambda i, j: (i % n_pos_blocks, 0)),
            pl.BlockSpec((tm, RET_DK // 2), lambda i, j: (i % n_pos_blocks, 0)),
        ],
        out_specs=pl.BlockSpec((tm, tn), lambda i, j: (i, j)),
        out_shape=jax.ShapeDtypeStruct((TOKENS, RET_IN_WIDTH), BF16),
        scratch_shapes=[pltpu.VMEM((tm, D_MODEL), BF16)],
        compiler_params=_params(("parallel", "arbitrary")),
        name="ret_inproj",
    )(h, gain, w, cos, sin)


RET_BLOCK = 256


def _retention_kernel(q_ref, k_ref, v_ref, g_ref, dec_ref, qd_ref, kd_ref, cd_ref, o_ref,
                      state_ref, acc_ref, upd_ref, *, n_chunks):
    @pl.when(pl.program_id(2) == 0)
    def _():
        state_ref[...] = jnp.zeros_like(state_ref)

    dec = dec_ref[...]
    qd = qd_ref[...]
    kd = kd_ref[...]
    cd = cd_ref[...]
    chunk = lambda c: slice(c * RET_BLOCK, (c + 1) * RET_BLOCK)

    for c in range(n_chunks):
        qc, kc, vc = q_ref[chunk(c), :], k_ref[chunk(c), :], v_ref[chunk(c), :]
        scores = lax.dot_general(qc, kc, (((1,), (1,)), ((), ())),
                                 preferred_element_type=F32) * dec
        acc_ref[chunk(c), :] = jnp.dot(scores.astype(BF16), vc, preferred_element_type=F32)
        k_dec = (kc.astype(F32) * kd).astype(BF16)
        upd_ref[c] = lax.dot_general(k_dec, vc, (((0,), (0,)), ((), ())),
                                     preferred_element_type=F32)

    for c in range(n_chunks):
        state = state_ref[...]
        cross = jnp.dot(q_ref[chunk(c), :], state.astype(BF16), preferred_element_type=F32) * qd
        state_ref[...] = state * cd + upd_ref[c]
        out = acc_ref[chunk(c), :] + cross
        ms = jnp.mean(out * out, axis=-1, keepdims=True)
        y = out * lax.rsqrt(ms + EPS) * g_ref[chunk(c), :].astype(F32)
        o_ref[chunk(c), :] = y.astype(o_ref.dtype)


def _retention(proj, dec, qd, kd, cd, *, rows=1024):
    n_row_blocks = SEQ // rows
    n_chunks = rows // RET_BLOCK
    k_off = RET_QK_WIDTH // RET_DK
    v_off = 2 * RET_QK_WIDTH // RET_DV
    g_off = v_off + RET_V_WIDTH // RET_DV
    row = lambda b, h, r: b * n_row_blocks + r
    kern = functools.partial(_retention_kernel, n_chunks=n_chunks)
    return pl.pallas_call(
        kern,
        grid=(BATCH, RET_HEADS, n_row_blocks),
        in_specs=[
            pl.BlockSpec((rows, RET_DK), lambda b, h, r: (row(b, h, r), h)),
            pl.BlockSpec((rows, RET_DK), lambda b, h, r: (row(b, h, r), k_off + h)),
            pl.BlockSpec((rows, RET_DV), lambda b, h, r: (row(b, h, r), v_off + h)),
            pl.BlockSpec((rows, RET_DV), lambda b, h, r: (row(b, h, r), g_off + h)),
            pl.BlockSpec((None, RET_BLOCK, RET_BLOCK), lambda b, h, r: (h, 0, 0)),
            pl.BlockSpec((None, RET_BLOCK, 1), lambda b, h, r: (h, 0, 0)),
            pl.BlockSpec((None, RET_BLOCK, 1), lambda b, h, r: (h, 0, 0)),
            pl.BlockSpec((None, 1, RET_DV), lambda b, h, r: (h, 0, 0)),
        ],
        out_specs=pl.BlockSpec((rows, RET_DV), lambda b, h, r: (row(b, h, r), h)),
        out_shape=jax.ShapeDtypeStruct((TOKENS, RET_V_WIDTH), BF16),
        scratch_shapes=[pltpu.VMEM((RET_DK, RET_DV), F32),
                        pltpu.VMEM((rows, RET_DV), F32),
                        pltpu.VMEM((n_chunks, RET_DK, RET_DV), F32)],
        compiler_params=_params(("parallel", "parallel", "arbitrary")),
        name="retention",
    )(proj, proj, proj, proj, dec, qd, kd, cd)


def _outproj_kernel(y_ref, w_ref, res_ref, o_ref):
    o_ref[...] = res_ref[...] + jnp.dot(y_ref[...], w_ref[...], preferred_element_type=F32)


def _outproj(y, w, res, *, tm=512, tn=1024):
    k = y.shape[1]
    return pl.pallas_call(
        _outproj_kernel,
        grid=(D_MODEL // tn, TOKENS // tm),
        in_specs=[
            pl.BlockSpec((tm, k), lambda j, i: (i, 0)),
            pl.BlockSpec((None, k, tn), lambda j, i: (0, 0, j)),
            pl.BlockSpec((tm, tn), lambda j, i: (i, j)),
        ],
        out_specs=pl.BlockSpec((tm, tn), lambda j, i: (i, j)),
        out_shape=jax.ShapeDtypeStruct((TOKENS, D_MODEL), F32),
        compiler_params=_params(("parallel", "parallel")),
        name="outproj",
    )(y, w, res)


def _mlp_kernel(h_ref, gain_ref, w1_ref, w2_ref, o_ref, hn_ref):
    @pl.when(pl.program_id(1) == 0)
    def _():
        h = h_ref[...]
        hn_ref[...] = _rms_norm_bf16(h, gain_ref[...])
        o_ref[...] = h

    a = jnp.dot(hn_ref[...], w1_ref[...], preferred_element_type=F32)
    act = jnp.square(jnp.maximum(a, 0.0)).astype(BF16)
    o_ref[...] += jnp.dot(act, w2_ref[...], preferred_element_type=F32)


def _mlp(h, gain, w1, w2, layer, *, tm=512, tf=1024):
    return pl.pallas_call(
        _mlp_kernel,
        grid=(TOKENS // tm, D_FF // tf),
        in_specs=[
            pl.BlockSpec((tm, D_MODEL), lambda i, f: (i, 0)),
            pl.BlockSpec((1, D_MODEL), lambda i, f: (0, 0)),
            pl.BlockSpec((None, D_MODEL, tf), lambda i, f: (layer, 0, f)),
            pl.BlockSpec((None, tf, D_MODEL), lambda i, f: (layer, f, 0)),
        ],
        out_specs=pl.BlockSpec((tm, D_MODEL), lambda i, f: (i, 0)),
        out_shape=jax.ShapeDtypeStruct((TOKENS, D_MODEL), F32),
        scratch_shapes=[pltpu.VMEM((tm, D_MODEL), BF16)],
        compiler_params=_params(("parallel", "arbitrary")),
        name="mlp",
    )(h, gain, w1, w2)


ATT_TILE = 256
ATT_SUB = 256


def _residue_major_matrix(dilation):
    r = jnp.arange(ATT_TILE)
    n = ATT_TILE // dilation
    src = (r % n) * dilation + r // n
    return (src[:, None] == r[None, :]).astype(BF16)


def _head_aux_matrix():
    i = jnp.arange(2 * ATT_HEAD_DIM)
    row, col = i[:, None], i[None, :]
    ones = (row < ATT_HEAD_DIM) & (col < ATT_HEAD_DIM)
    r, c = row - ATT_HEAD_DIM, col - ATT_HEAD_DIM
    minus = (c >= 0) & (c < ROPE_HALF) & (r == c + ROPE_HALF)
    plus = (r >= 0) & (r < ROPE_HALF) & (c == r + ROPE_HALF)
    return (ones.astype(F32) + plus.astype(F32) - minus.astype(F32)).astype(BF16)


def _att_inproj_kernel(h_ref, gain_ref, w_ref, qg_ref, kg_ref, cos_ref, sin_ref, aux_ref,
                       p1_ref, p2_ref, o_ref, hn_ref, *, tiles_per_unit, n_q_tiles, n_qk_tiles):
    j = pl.program_id(1)
    tm, tn = o_ref.shape

    @pl.when(j == 0)
    def _():
        hn_ref[0] = _rms_norm_bf16(h_ref[...], gain_ref[...])
        for g, p_ref in ((1, p1_ref), (2, p2_ref)):
            for t in range(tm // ATT_TILE):
                rows = slice(t * ATT_TILE, (t + 1) * ATT_TILE)
                hn_ref[g, rows, :] = jnp.dot(p_ref[...], hn_ref[0, rows, :],
                                             preferred_element_type=F32).astype(BF16)

    g = (j // tiles_per_unit) % N_ATT_GROUPS

    n_sub = tn // ATT_SUB

    def slab(k):
        return jnp.dot(hn_ref[g], w_ref[:, k * ATT_SUB:(k + 1) * ATT_SUB],
                       preferred_element_type=F32)

    @pl.when(j >= n_qk_tiles)
    def _():
        for k in range(n_sub):
            o_ref[:, k * ATT_SUB:(k + 1) * ATT_SUB] = slab(k).astype(o_ref.dtype)

    @pl.when(j < n_qk_tiles)
    def _():
        is_q = j < n_q_tiles
        head_gain = jnp.where(is_q, qg_ref[...] * (ATT_HEAD_DIM ** -0.5 * LOG2_E), kg_ref[...])
        c = cos_ref[...]
        s = sin_ref[...]

        def epilogue(k, acc):
            for hh in range(ATT_SUB // ATT_HEAD_DIM):
                x = acc[:, hh * ATT_HEAD_DIM:(hh + 1) * ATT_HEAD_DIM]
                xg = x * head_gain
                lhs = jnp.concatenate([(x * x).astype(BF16), xg.astype(BF16)], axis=1)
                aux = jnp.dot(lhs, aux_ref[...], preferred_element_type=F32)
                inv = lax.rsqrt(aux[:, :ATT_HEAD_DIM] * (1.0 / ATT_HEAD_DIM) + EPS)
                rot = (xg * c + aux[:, ATT_HEAD_DIM:] * s) * inv
                lo = k * ATT_SUB + hh * ATT_HEAD_DIM
                o_ref[:, lo:lo + ATT_HEAD_DIM] = rot.astype(o_ref.dtype)

        acc = slab(0)
        for k in range(n_sub):
            nxt = slab(k + 1) if k + 1 < n_sub else None
            epilogue(k, acc)
            acc = nxt


def _att_inproj(h, gain, w, q_gain, k_gain, cos, sin, *, tm=1024, tn=1024):
    n_pos_blocks = SEQ // tm
    group_width = N_ATT_GROUPS * ATT_WIDTH
    tiles_per_unit = ATT_WIDTH // tn
    kern = functools.partial(
        _att_inproj_kernel, tiles_per_unit=tiles_per_unit, n_q_tiles=group_width // tn,
        n_qk_tiles=2 * group_width // tn)
    pos_spec = pl.BlockSpec(
        (None, tm, ATT_HEAD_DIM),
        lambda i, j: ((j // tiles_per_unit) % N_ATT_GROUPS, i % n_pos_blocks, 0))
    return pl.pallas_call(
        kern,
        grid=(TOKENS // tm, ATT_IN_WIDTH // tn),
        in_specs=[
            pl.BlockSpec((tm, D_MODEL), lambda i, j: (i, 0)),
            pl.BlockSpec((1, D_MODEL), lambda i, j: (0, 0)),
            pl.BlockSpec((None, D_MODEL, tn), lambda i, j: (0, 0, j)),
            pl.BlockSpec((1, ATT_HEAD_DIM), lambda i, j: (0, 0)),
            pl.BlockSpec((1, ATT_HEAD_DIM), lambda i, j: (0, 0)),
            pos_spec, pos_spec,
            pl.BlockSpec((2 * ATT_HEAD_DIM, 2 * ATT_HEAD_DIM), lambda i, j: (0, 0)),
            pl.BlockSpec((ATT_TILE, ATT_TILE), lambda i, j: (0, 0)),
            pl.BlockSpec((ATT_TILE, ATT_TILE), lambda i, j: (0, 0)),
        ],
        out_specs=pl.BlockSpec((tm, tn), lambda i, j: (i, j)),
        out_shape=jax.ShapeDtypeStruct((TOKENS, ATT_IN_WIDTH), BF16),
        scratch_shapes=[pltpu.VMEM((N_ATT_GROUPS, tm, D_MODEL), BF16)],
        compiler_params=_params(("parallel", "arbitrary")),
        name="att_inproj",
    )(h, gain, w, q_gain, k_gain, cos, sin, _head_aux_matrix(),
      _residue_major_matrix(ATT_GROUPS[1][1]), _residue_major_matrix(ATT_GROUPS[2][1]))


ATT_STEP = 2
LOG2_E = 1.4426950408889634
LN_2 = 0.6931471805599453


def _attention_kernel(q_ref, k_ref, v_ref, o_ref, lse_ref, kp_ref, vp_ref, s_ref):
    n = pl.program_id(2)

    @pl.when(n == 0)
    def _():
        kp_ref[...] = jnp.zeros_like(kp_ref)
        vp_ref[...] = jnp.zeros_like(vp_ref)

    def block_rows(ref, b):
        if len(ref.shape) == 2:
            return (slice(b * ATT_BLOCK, (b + 1) * ATT_BLOCK),)
        per = ref.shape[0] // ATT_STEP
        return (slice(b * per, (b + 1) * per), slice(None))

    def load(ref, b, cols):
        x = ref[block_rows(ref, b) + (cols,)]
        return x.reshape(ATT_BLOCK, x.shape[-1])

    def store(ref, b, cols, value):
        idx = block_rows(ref, b) + (cols,)
        ref[idx] = value.reshape(ref[idx].shape)

    qi = lax.broadcasted_iota(jnp.int32, (ATT_BLOCK, 2 * ATT_BLOCK), 0)
    kj = lax.broadcasted_iota(jnp.int32, (ATT_BLOCK, 2 * ATT_BLOCK), 1)
    band = (kj >= qi) & (kj <= qi + ATT_BLOCK)
    first_band = band & ((n > 0) | (kj >= ATT_BLOCK))
    lane = lax.broadcasted_iota(jnp.int32, (ATT_BLOCK, LANES), 1)
    contract_last = (((1,), (1,)), ((), ()))
    head_cols = [slice(h * ATT_HEAD_DIM, (h + 1) * ATT_HEAD_DIM) for h in range(ATT_HEADS)]

    for b in range(ATT_STEP):
        for h, cols in enumerate(head_cols):
            q = load(q_ref, b, cols)
            k_prev = kp_ref[:, cols] if b == 0 else load(k_ref, b - 1, cols)
            s_ref[b * ATT_HEADS + h, :, :ATT_BLOCK] = lax.dot_general(
                q, k_prev, contract_last, preferred_element_type=F32)
            s_ref[b * ATT_HEADS + h, :, ATT_BLOCK:] = lax.dot_general(
                q, load(k_ref, b, cols), contract_last, preferred_element_type=F32)

    for b in range(ATT_STEP):
        valid = first_band if b == 0 else band
        max_tile = jnp.zeros((ATT_BLOCK, LANES), F32)
        sum_tile = jnp.ones((ATT_BLOCK, LANES), F32)
        for h, cols in enumerate(head_cols):
            s = jnp.where(valid, s_ref[b * ATT_HEADS + h], -jnp.inf)
            m = jnp.max(s, axis=-1, keepdims=True)
            p = jnp.exp2(s - m)
            denom = jnp.sum(p, axis=-1, keepdims=True)
            pb = p.astype(BF16)
            v_prev = vp_ref[:, cols] if b == 0 else load(v_ref, b - 1, cols)
            pv = (jnp.dot(pb[:, :ATT_BLOCK], v_prev, preferred_element_type=F32)
                  + jnp.dot(pb[:, ATT_BLOCK:], load(v_ref, b, cols), preferred_element_type=F32))
            store(o_ref, b, cols, (pv / denom).astype(o_ref.dtype))
            max_tile = jnp.where(lane == h, m, max_tile)
            sum_tile = jnp.where(lane == h, denom, sum_tile)
        store(lse_ref, b, slice(None), (max_tile + jnp.log2(sum_tile)) * LN_2)

    last = ATT_STEP - 1
    kp_ref[...] = load(k_ref, last, slice(None))
    vp_ref[...] = load(v_ref, last, slice(None))


def _attention_group(qkv, group, dilation):
    n_sub = ATT_TILE // dilation
    step = ATT_STEP * ATT_BLOCK
    n_steps = SEQ // dilation // step
    n_tiles = TOKENS // ATT_TILE
    tiles_per_batch = SEQ // ATT_TILE
    q_unit, k_unit, v_unit = group, N_ATT_GROUPS + group, 2 * N_ATT_GROUPS + group
    view = qkv.reshape(n_tiles, dilation, n_sub, ATT_IN_WIDTH)

    if n_sub >= step:
        per_tile = n_sub // step
        lead = (None, None, step)
        index = lambda b, c, n: (b * tiles_per_batch + n // per_tile, c, n % per_tile)
    else:
        span = step // n_sub
        lead = (span, None, n_sub)
        index = lambda b, c, n: (b * (tiles_per_batch // span) + n, c, 0)

    def spec(width, unit):
        return pl.BlockSpec(lead + (width,), lambda b, c, n: index(b, c, n) + (unit,))

    return pl.pallas_call(
        _attention_kernel,
        grid=(BATCH, dilation, n_steps),
        in_specs=[spec(ATT_WIDTH, q_unit), spec(ATT_WIDTH, k_unit), spec(ATT_WIDTH, v_unit)],
        out_specs=[spec(ATT_WIDTH, 0), spec(LANES, 0)],
        out_shape=[
            jax.ShapeDtypeStruct((n_tiles, dilation, n_sub, ATT_WIDTH), BF16),
            jax.ShapeDtypeStruct((n_tiles, dilation, n_sub, LANES), F32),
        ],
        scratch_shapes=[
            pltpu.VMEM((ATT_BLOCK, ATT_WIDTH), BF16),
            pltpu.VMEM((ATT_BLOCK, ATT_WIDTH), BF16),
            pltpu.VMEM((ATT_STEP * ATT_HEADS, ATT_BLOCK, 2 * ATT_BLOCK), F32),
        ],
        compiler_params=_params(("parallel", "parallel", "arbitrary")),
        name=f"attention_d{dilation}",
    )(view, view, view)


OUT_TM = 256


def _att_outproj_kernel(o0_ref, o1_ref, o2_ref, l0_ref, l1_ref, l2_ref, p1_ref, p2_ref, w_ref,
                        res_ref, out_ref, lse_ref, even_ref, odd_ref):
    s = pl.program_id(0)

    @pl.when(s == 0)
    def _():
        odd_ref[...] = jnp.zeros_like(odd_ref)

    def step(merged_prev_ref, merged_ref):
        out_ref[...] = res_ref[...] + jnp.dot(merged_prev_ref[...], w_ref[...],
                                              preferred_element_type=F32)
        for k, l_ref in enumerate((l1_ref, l2_ref)):
            d, n = l_ref.shape[0], l_ref.shape[1]
            for c in range(d):
                lse_ref[k, pl.ds(c, n, stride=d), :] = l_ref[c]
        l0, l1, l2 = l0_ref[0], lse_ref[0], lse_ref[1]
        top = jnp.maximum(jnp.maximum(l0, l1), l2)
        e0, e1, e2 = jnp.exp(l0 - top), jnp.exp(l1 - top), jnp.exp(l2 - top)
        total = e0 + e1 + e2
        w0, w1, w2 = e0 / total, e1 / total, e2 / total
        o1 = jnp.dot(p1_ref[...], o1_ref[...].reshape(OUT_TM, ATT_WIDTH),
                     preferred_element_type=F32)
        o2 = jnp.dot(p2_ref[...], o2_ref[...].reshape(OUT_TM, ATT_WIDTH),
                     preferred_element_type=F32)
        for h in range(ATT_HEADS):
            cols = slice(h * ATT_HEAD_DIM, (h + 1) * ATT_HEAD_DIM)
            merged = (w0[:, h:h + 1] * o0_ref[0, :, cols].astype(F32)
                      + w1[:, h:h + 1] * o1[:, cols]
                      + w2[:, h:h + 1] * o2[:, cols])
            merged_ref[:, cols] = merged.astype(BF16)

    @pl.when(s % 2 == 0)
    def _():
        step(odd_ref, even_ref)

    @pl.when(s % 2 == 1)
    def _():
        step(even_ref, odd_ref)


def _token_order_matrix(dilation):
    t = jnp.arange(OUT_TM)
    src = (t % dilation) * (OUT_TM // dilation) + t // dilation
    return (src[:, None] == t[None, :]).astype(BF16)


def _att_outproj(outs, lses, w, res):
    tm = OUT_TM
    per_tile = ATT_TILE // tm

    n_row_tiles = TOKENS // tm
    merge_tile = lambda s: jnp.minimum(s, n_row_tiles - 1)
    project_tile = lambda s: jnp.maximum(s - 1, 0)

    def spec(dilation, width):
        return pl.BlockSpec(
            (None, dilation, tm // dilation, width),
            lambda s: (merge_tile(s) // per_tile, 0, merge_tile(s) % per_tile, 0))

    dils = [d for _, d in ATT_GROUPS]
    const = lambda shape: pl.BlockSpec(shape, lambda s: (0, 0))
    return pl.pallas_call(
        _att_outproj_kernel,
        grid=(n_row_tiles + 1,),
        in_specs=[spec(d, ATT_WIDTH) for d in dils] + [spec(d, LANES) for d in dils] + [
            const((tm, tm)), const((tm, tm)),
            pl.BlockSpec((None, ATT_WIDTH, D_MODEL), lambda s: (0, 0, 0)),
            pl.BlockSpec((tm, D_MODEL), lambda s: (project_tile(s), 0))],
        out_specs=pl.BlockSpec((tm, D_MODEL), lambda s: (project_tile(s), 0)),
        out_shape=jax.ShapeDtypeStruct((TOKENS, D_MODEL), F32),
        scratch_shapes=[pltpu.VMEM((N_ATT_GROUPS - 1, tm, LANES), F32),
                        pltpu.VMEM((tm, ATT_WIDTH), BF16),
                        pltpu.VMEM((tm, ATT_WIDTH), BF16)],
        compiler_params=_params(("arbitrary",)),
        name="att_outproj",
    )(*outs, *lses, _token_order_matrix(dils[1]), _token_order_matrix(dils[2]), w, res)


def _retention_tables():
    pos = jnp.arange(SEQ, dtype=F32)
    inv_freq = 1.0 / (RET_ROT_BASE ** jnp.linspace(0.0, 1.0, RET_DK // 2, dtype=F32))
    ang = pos[:, None] * inv_freq[None, :]
    log_gamma = jnp.log(1.0 - 2.0 ** (-5.0 - jnp.arange(RET_HEADS, dtype=F32)))
    idx = jnp.arange(RET_BLOCK, dtype=F32)
    diff = idx[:, None] - idx[None, :]
    inner_decay = jnp.where(diff >= 0,
                            jnp.exp(log_gamma[:, None, None] * jnp.maximum(diff, 0.0)), 0.0)
    query_decay = jnp.exp(log_gamma[:, None] * (idx[None, :] + 1.0))[:, :, None]
    key_decay = jnp.exp(log_gamma[:, None] * (RET_BLOCK - 1.0 - idx[None, :]))[:, :, None]
    chunk_decay = jnp.broadcast_to(jnp.exp(log_gamma * RET_BLOCK)[:, None, None],
                                   (RET_HEADS, 1, RET_DV))
    return jnp.cos(ang), jnp.sin(ang), inner_decay, query_decay, key_decay, chunk_decay


def _attention_tables():
    pos = jnp.arange(SEQ, dtype=F32)
    inv_freq = ROPE_THETA ** (-jnp.arange(0, ROPE_DIM, 2, dtype=F32) / ROPE_DIM)
    ang = pos[:, None] * inv_freq[None, :]
    cos, sin = jnp.cos(ang), jnp.sin(ang)
    zeros = jnp.zeros((SEQ, ATT_HEAD_DIM - ROPE_DIM), F32)
    cos_full = jnp.concatenate([cos, cos, jnp.ones_like(zeros)], axis=1)
    sin_full = jnp.concatenate([sin, sin, zeros], axis=1)

    def per_group(table):
        copies = []
        for _, d in ATT_GROUPS:
            t = table.reshape(SEQ // ATT_TILE, ATT_TILE // d, d, ATT_HEAD_DIM)
            copies.append(t.swapaxes(1, 2).reshape(SEQ, ATT_HEAD_DIM))
        return jnp.stack(copies)

    return per_group(cos_full), per_group(sin_full)


def kernel(x, norm_mix_gain, norm_mlp_gain, ret_w_in, ret_w_out, att_w_in, att_q_gain,
           att_k_gain, att_w_out, mlp_w_in, mlp_w_out):
    h = x.reshape(TOKENS, D_MODEL)
    mlp_w1 = mlp_w_in.astype(BF16)
    mlp_w2 = mlp_w_out.astype(BF16)

    cos, sin, inner_decay, query_decay, key_decay, chunk_decay = _retention_tables()
    proj = _ret_inproj(h, norm_mix_gain[0:1], ret_w_in.astype(BF16), cos, sin)
    y = _retention(proj, inner_decay, query_decay, key_decay, chunk_decay)
    h = _outproj(y, ret_w_out.astype(BF16), h)
    h = _mlp(h, norm_mlp_gain[0:1], mlp_w1, mlp_w2, 0)

    cos_full, sin_full = _attention_tables()
    qkv = _att_inproj(h, norm_mix_gain[1:2], att_w_in.astype(BF16), att_q_gain[0:1],
                      att_k_gain[0:1], cos_full, sin_full)
    outs, lses = [], []
    for g, (_, dilation) in enumerate(ATT_GROUPS):
        o, l = _attention_group(qkv, g, dilation)
        outs.append(o)
        lses.append(l)
    h = _att_outproj(outs, lses, att_w_out.astype(BF16), h)
    h = _mlp(h, norm_mlp_gain[1:2], mlp_w1, mlp_w2, 1)
    return h.reshape(BATCH, SEQ, D_MODEL)
```

```python
import functools

import jax
import jax.numpy as jnp
from jax import lax
from jax.experimental import pallas as pl
from jax.experimental.pallas import tpu as pltpu

F32 = jnp.float32
BF16 = jnp.bfloat16

D_MODEL = 2048
BATCH = 2
SEQ = 8192
TOKENS = BATCH * SEQ
EPS = 1e-6

RET_HEADS = 8
RET_DK = 256
RET_DV = 512
RET_QK_WIDTH = RET_HEADS * RET_DK
RET_V_WIDTH = RET_HEADS * RET_DV
RET_IN_WIDTH = 2 * RET_QK_WIDTH + 2 * RET_V_WIDTH
RET_CHUNK = 128
RET_ROT_BASE = 10000.0

ATT_GROUPS = ((128, 1), (512, 4), (2048, 16))
N_ATT_GROUPS = 3
ATT_HEAD_DIM = 128
ATT_HEADS = 16
ATT_WIDTH = ATT_HEADS * ATT_HEAD_DIM
ATT_IN_WIDTH = 3 * N_ATT_GROUPS * ATT_WIDTH
ATT_BLOCK = 128
ROPE_DIM = 32
ROPE_HALF = ROPE_DIM // 2
ROPE_THETA = 500000.0

D_FF = 4 * D_MODEL

LANES = 128
VMEM_LIMIT = 56 * 1024 * 1024


def _params(semantics):
    return pltpu.CompilerParams(dimension_semantics=semantics, vmem_limit_bytes=VMEM_LIMIT)


def _rms_norm_bf16(x, gain):
    ms = jnp.mean(x * x, axis=-1, keepdims=True)
    return (x * lax.rsqrt(ms + EPS) * gain).astype(BF16)


SIDE_CAST_BLOCKS = 128


def _side_cast(weights, step_of):
    operands, in_specs, out_specs, out_shapes = [], [], [], []
    for arr, layer in weights:
        _, rows, cols = arr.shape
        block = (None, rows // SIDE_CAST_BLOCKS, cols)
        blk = lambda *ids: jnp.minimum(step_of(*ids), SIDE_CAST_BLOCKS - 1)
        operands.append(arr)
        in_specs.append(pl.BlockSpec(block, lambda *ids, layer=layer, blk=blk: (layer, blk(*ids), 0)))
        out_specs.append(pl.BlockSpec(block, lambda *ids, blk=blk: (0, blk(*ids), 0)))
        out_shapes.append(jax.ShapeDtypeStruct((1, rows, cols), BF16))
    return operands, in_specs, out_specs, out_shapes


def _run_side_casts(src_refs, dst_refs):
    for src, dst in zip(src_refs, dst_refs):
        dst[...] = src[...].astype(BF16)


def _ret_inproj_kernel(h_ref, gain_ref, w_ref, cos_ref, sin_ref, *rest,
                       n_side, n_q_tiles, n_rot_tiles, n_plain_tiles, heads_per_tile):
    side_in, o_ref, side_out, hn_ref = (rest[:n_side], rest[n_side],
                                        rest[n_side + 1:2 * n_side + 1], rest[2 * n_side + 1])
    j = pl.program_id(1)

    @pl.when(j == 0)
    def _():
        hn_ref[...] = _rms_norm_bf16(h_ref[...], gain_ref[...])

    def slab(hh):
        cols = slice(hh * RET_DK, (hh + 1) * RET_DK)
        return jnp.dot(hn_ref[...], w_ref[:, cols], preferred_element_type=F32)

    @pl.when((j >= n_rot_tiles) & (j < n_plain_tiles))
    def _():
        _run_side_casts(side_in, side_out)
        for hh in range(heads_per_tile):
            o_ref[:, hh * RET_DK:(hh + 1) * RET_DK] = slab(hh).astype(o_ref.dtype)

    @pl.when(j >= n_plain_tiles)
    def _():
        _run_side_casts(side_in, side_out)
        for hh in range(heads_per_tile):
            gate = slab(hh)
            o_ref[:, hh * RET_DK:(hh + 1) * RET_DK] = (
                gate * jax.nn.sigmoid(gate)).astype(o_ref.dtype)

    @pl.when(j < n_rot_tiles)
    def _():
        _run_side_casts(side_in, side_out)
        scale = jnp.where(j >= n_q_tiles, RET_DK ** -0.5, 1.0).astype(F32)
        c = cos_ref[...] * scale
        s = sin_ref[...] * scale
        half = RET_DK // 2
        for hh in range(heads_per_tile):
            acc = slab(hh)
            lo = hh * RET_DK
            x1 = acc[:, :half]
            x2 = acc[:, half:]
            o_ref[:, lo:lo + half] = (x1 * c - x2 * s).astype(o_ref.dtype)
            o_ref[:, lo + half:lo + RET_DK] = (x2 * c + x1 * s).astype(o_ref.dtype)


def _ret_inproj(h, gain, w, cos, sin, side_weights, *, tm=1024, tn=1024):
    n_pos_blocks = SEQ // tm
    n_col_tiles = RET_IN_WIDTH // tn
    side_ops, side_in, side_out, side_shapes = _side_cast(
        side_weights, lambda i, j: i * n_col_tiles + j)
    kern = functools.partial(
        _ret_inproj_kernel, n_side=len(side_ops), n_q_tiles=RET_QK_WIDTH // tn,
        n_rot_tiles=2 * RET_QK_WIDTH // tn,
        n_plain_tiles=(2 * RET_QK_WIDTH + RET_V_WIDTH) // tn, heads_per_tile=tn // RET_DK)
    return pl.pallas_call(
        kern,
        grid=(TOKENS // tm, n_col_tiles),
        in_specs=[
            pl.BlockSpec((tm, D_MODEL), lambda i, j: (i, 0)),
            pl.BlockSpec((1, D_MODEL), lambda i, j: (0, 0)),
            pl.BlockSpec((None, D_MODEL, tn), lambda i, j: (0, 0, j)),
            pl.BlockSpec((tm, RET_DK // 2), lambda i, j: (i % n_pos_blocks, 0)),
            pl.BlockSpec((tm, RET_DK // 2), lambda i, j: (i % n_pos_blocks, 0)),
        ] + side_in,
        out_specs=[pl.BlockSpec((tm, tn), lambda i, j: (i, j))] + side_out,
        out_shape=[jax.ShapeDtypeStruct((TOKENS, RET_IN_WIDTH), BF16)] + side_shapes,
        scratch_shapes=[pltpu.VMEM((tm, D_MODEL), BF16)],
        compiler_params=_params(("arbitrary", "arbitrary")),
        name="ret_inproj",
    )(h, gain, w, cos, sin, *side_ops)


RET_BLOCK = 256


def _retention_kernel(q_ref, k_ref, v_ref, g_ref, dec_ref, qd_ref, kd_ref, cd_ref, o_ref,
                      state_ref, acc_ref, upd_ref, *, n_chunks):
    @pl.when(pl.program_id(2) == 0)
    def _():
        state_ref[...] = jnp.zeros_like(state_ref)

    dec = dec_ref[...]
    qd = qd_ref[...]
    kd = kd_ref[...]
    cd = cd_ref[...]
    chunk = lambda c: slice(c * RET_BLOCK, (c + 1) * RET_BLOCK)

    for c in range(n_chunks):
        qc, kc, vc = q_ref[chunk(c), :], k_ref[chunk(c), :], v_ref[chunk(c), :]
        scores = lax.dot_general(qc, kc, (((1,), (1,)), ((), ())),
                                 preferred_element_type=F32) * dec
        acc_ref[chunk(c), :] = jnp.dot(scores.astype(BF16), vc, preferred_element_type=F32)
        k_dec = (kc.astype(F32) * kd).astype(BF16)
        upd_ref[c] = lax.dot_general(k_dec, vc, (((0,), (0,)), ((), ())),
                                     preferred_element_type=F32)

    for c in range(n_chunks):
        state = state_ref[...]
        cross = jnp.dot(q_ref[chunk(c), :], state.astype(BF16), preferred_element_type=F32) * qd
        state_ref[...] = state * cd + upd_ref[c]
        out = acc_ref[chunk(c), :] + cross
        ms = jnp.mean(out * out, axis=-1, keepdims=True)
        y = out * lax.rsqrt(ms + EPS) * g_ref[chunk(c), :].astype(F32)
        o_ref[chunk(c), :] = y.astype(o_ref.dtype)


def _retention(proj, dec, qd, kd, cd, *, rows=1024):
    n_row_blocks = SEQ // rows
    n_chunks = rows // RET_BLOCK
    k_off = RET_QK_WIDTH // RET_DK
    v_off = 2 * RET_QK_WIDTH // RET_DV
    g_off = v_off + RET_V_WIDTH // RET_DV
    row = lambda b, h, r: b * n_row_blocks + r
    kern = functools.partial(_retention_kernel, n_chunks=n_chunks)
    return pl.pallas_call(
        kern,
        grid=(BATCH, RET_HEADS, n_row_blocks),
        in_specs=[
            pl.BlockSpec((rows, RET_DK), lambda b, h, r: (row(b, h, r), h)),
            pl.BlockSpec((rows, RET_DK), lambda b, h, r: (row(b, h, r), k_off + h)),
            pl.BlockSpec((rows, RET_DV), lambda b, h, r: (row(b, h, r), v_off + h)),
            pl.BlockSpec((rows, RET_DV), lambda b, h, r: (row(b, h, r), g_off + h)),
            pl.BlockSpec((None, RET_BLOCK, RET_BLOCK), lambda b, h, r: (h, 0, 0)),
            pl.BlockSpec((None, RET_BLOCK, 1), lambda b, h, r: (h, 0, 0)),
            pl.BlockSpec((None, RET_BLOCK, 1), lambda b, h, r: (h, 0, 0)),
            pl.BlockSpec((None, 1, RET_DV), lambda b, h, r: (h, 0, 0)),
        ],
        out_specs=pl.BlockSpec((rows, RET_DV), lambda b, h, r: (row(b, h, r), h)),
        out_shape=jax.ShapeDtypeStruct((TOKENS, RET_V_WIDTH), BF16),
        scratch_shapes=[pltpu.VMEM((RET_DK, RET_DV), F32),
                        pltpu.VMEM((rows, RET_DV), F32),
                        pltpu.VMEM((n_chunks, RET_DK, RET_DV), F32)],
        compiler_params=_params(("parallel", "parallel", "arbitrary")),
        name="retention",
    )(proj, proj, proj, proj, dec, qd, kd, cd)


def _outproj_kernel(y_ref, w_ref, res_ref, o_ref):
    o_ref[...] = res_ref[...] + jnp.dot(y_ref[...], w_ref[...], preferred_element_type=F32)


def _outproj(y, w, res, *, tm=512, tn=1024):
    k = y.shape[1]
    return pl.pallas_call(
        _outproj_kernel,
        grid=(D_MODEL // tn, TOKENS // tm),
        in_specs=[
            pl.BlockSpec((tm, k), lambda j, i: (i, 0)),
            pl.BlockSpec((None, k, tn), lambda j, i: (0, 0, j)),
            pl.BlockSpec((tm, tn), lambda j, i: (i, j)),
        ],
        out_specs=pl.BlockSpec((tm, tn), lambda j, i: (i, j)),
        out_shape=jax.ShapeDtypeStruct((TOKENS, D_MODEL), F32),
        compiler_params=_params(("parallel", "parallel")),
        name="outproj",
    )(y, w, res)


def _mlp_kernel(h_ref, gain_ref, w1_ref, w2_ref, *rest, n_side):
    side_in, o_ref, side_out, hn_ref = (rest[:n_side], rest[n_side],
                                        rest[n_side + 1:2 * n_side + 1], rest[2 * n_side + 1])
    @pl.when(pl.program_id(1) == 0)
    def _():
        h = h_ref[...]
        hn_ref[...] = _rms_norm_bf16(h, gain_ref[...])
        o_ref[...] = h

    _run_side_casts(side_in, side_out)
    a = jnp.dot(hn_ref[...], w1_ref[...], preferred_element_type=F32)
    act = jnp.square(jnp.maximum(a, 0.0)).astype(BF16)
    o_ref[...] += jnp.dot(act, w2_ref[...], preferred_element_type=F32)


def _mlp(h, gain, w1, w2, side_weights=(), *, tm=512, tf=1024):
    n_ff_tiles = D_FF // tf
    side_ops, side_in, side_out, side_shapes = _side_cast(
        side_weights, lambda i, f: i * n_ff_tiles + f)
    return pl.pallas_call(
        functools.partial(_mlp_kernel, n_side=len(side_ops)),
        grid=(TOKENS // tm, n_ff_tiles),
        in_specs=[
            pl.BlockSpec((tm, D_MODEL), lambda i, f: (i, 0)),
            pl.BlockSpec((1, D_MODEL), lambda i, f: (0, 0)),
            pl.BlockSpec((None, D_MODEL, tf), lambda i, f: (0, 0, f)),
            pl.BlockSpec((None, tf, D_MODEL), lambda i, f: (0, f, 0)),
        ] + side_in,
        out_specs=[pl.BlockSpec((tm, D_MODEL), lambda i, f: (i, 0))] + side_out,
        out_shape=[jax.ShapeDtypeStruct((TOKENS, D_MODEL), F32)] + side_shapes,
        scratch_shapes=[pltpu.VMEM((tm, D_MODEL), BF16)],
        compiler_params=_params(("arbitrary", "arbitrary")),
        name="mlp",
    )(h, gain, w1, w2, *side_ops)


ATT_TILE = 256
ATT_SUB = 256


def _residue_major_matrix(dilation):
    r = jnp.arange(ATT_TILE)
    n = ATT_TILE // dilation
    src = (r % n) * dilation + r // n
    return (src[:, None] == r[None, :]).astype(BF16)


def _head_aux_matrix():
    i = jnp.arange(2 * ATT_HEAD_DIM)
    row, col = i[:, None], i[None, :]
    ones = (row < ATT_HEAD_DIM) & (col < ATT_HEAD_DIM)
    r, c = row - ATT_HEAD_DIM, col - ATT_HEAD_DIM
    minus = (c >= 0) & (c < ROPE_HALF) & (r == c + ROPE_HALF)
    plus = (r >= 0) & (r < ROPE_HALF) & (c == r + ROPE_HALF)
    return (ones.astype(F32) + plus.astype(F32) - minus.astype(F32)).astype(BF16)


def _att_inproj_kernel(h_ref, gain_ref, w_ref, qg_ref, kg_ref, cos_ref, sin_ref, aux_ref,
                       p1_ref, p2_ref, o_ref, hn_ref, *, tiles_per_unit, n_q_tiles, n_qk_tiles):
    j = pl.program_id(1)
    tm, tn = o_ref.shape

    @pl.when(j == 0)
    def _():
        hn_ref[0] = _rms_norm_bf16(h_ref[...], gain_ref[...])
        for g, p_ref in ((1, p1_ref), (2, p2_ref)):
            for t in range(tm // ATT_TILE):
                rows = slice(t * ATT_TILE, (t + 1) * ATT_TILE)
                hn_ref[g, rows, :] = jnp.dot(p_ref[...], hn_ref[0, rows, :],
                                             preferred_element_type=F32).astype(BF16)

    g = (j // tiles_per_unit) % N_ATT_GROUPS

    n_sub = tn // ATT_SUB

    def slab(k):
        return jnp.dot(hn_ref[g], w_ref[:, k * ATT_SUB:(k + 1) * ATT_SUB],
                       preferred_element_type=F32)

    @pl.when(j >= n_qk_tiles)
    def _():
        for k in range(n_sub):
            o_ref[:, k * ATT_SUB:(k + 1) * ATT_SUB] = slab(k).astype(o_ref.dtype)

    @pl.when(j < n_qk_tiles)
    def _():
        is_q = j < n_q_tiles
        head_gain = jnp.where(is_q, qg_ref[...] * (ATT_HEAD_DIM ** -0.5 * LOG2_E), kg_ref[...])
        c = cos_ref[...]
        s = sin_ref[...]

        def epilogue(k, acc):
            for hh in range(ATT_SUB // ATT_HEAD_DIM):
                x = acc[:, hh * ATT_HEAD_DIM:(hh + 1) * ATT_HEAD_DIM]
                xg = x * head_gain
                lhs = jnp.concatenate([(x * x).astype(BF16), xg.astype(BF16)], axis=1)
                aux = jnp.dot(lhs, aux_ref[...], preferred_element_type=F32)
                inv = lax.rsqrt(aux[:, :ATT_HEAD_DIM] * (1.0 / ATT_HEAD_DIM) + EPS)
                rot = (xg * c + aux[:, ATT_HEAD_DIM:] * s) * inv
                lo = k * ATT_SUB + hh * ATT_HEAD_DIM
                o_ref[:, lo:lo + ATT_HEAD_DIM] = rot.astype(o_ref.dtype)

        acc = slab(0)
        for k in range(n_sub):
            nxt = slab(k + 1) if k + 1 < n_sub else None
            epilogue(k, acc)
            acc = nxt


def _att_inproj(h, gain, w, q_gain, k_gain, cos, sin, *, tm=1024, tn=1024):
    n_pos_blocks = SEQ // tm
    group_width = N_ATT_GROUPS * ATT_WIDTH
    tiles_per_unit = ATT_WIDTH // tn
    kern = functools.partial(
        _att_inproj_kernel, tiles_per_unit=tiles_per_unit, n_q_tiles=group_width // tn,
        n_qk_tiles=2 * group_width // tn)
    pos_spec = pl.BlockSpec(
        (None, tm, ATT_HEAD_DIM),
        lambda i, j: ((j // tiles_per_unit) % N_ATT_GROUPS, i % n_pos_blocks, 0))
    return pl.pallas_call(
        kern,
        grid=(TOKENS // tm, ATT_IN_WIDTH // tn),
        in_specs=[
            pl.BlockSpec((tm, D_MODEL), lambda i, j: (i, 0)),
            pl.BlockSpec((1, D_MODEL), lambda i, j: (0, 0)),
            pl.BlockSpec((None, D_MODEL, tn), lambda i, j: (0, 0, j)),
            pl.BlockSpec((1, ATT_HEAD_DIM), lambda i, j: (0, 0)),
            pl.BlockSpec((1, ATT_HEAD_DIM), lambda i, j: (0, 0)),
            pos_spec, pos_spec,
            pl.BlockSpec((2 * ATT_HEAD_DIM, 2 * ATT_HEAD_DIM), lambda i, j: (0, 0)),
            pl.BlockSpec((ATT_TILE, ATT_TILE), lambda i, j: (0, 0)),
            pl.BlockSpec((ATT_TILE, ATT_TILE), lambda i, j: (0, 0)),
        ],
        out_specs=pl.BlockSpec((tm, tn), lambda i, j: (i, j)),
        out_shape=jax.ShapeDtypeStruct((TOKENS, ATT_IN_WIDTH), BF16),
        scratch_shapes=[pltpu.VMEM((N_ATT_GROUPS, tm, D_MODEL), BF16)],
        compiler_params=_params(("parallel", "arbitrary")),
        name="att_inproj",
    )(h, gain, w, q_gain, k_gain, cos, sin, _head_aux_matrix(),
      _residue_major_matrix(ATT_GROUPS[1][1]), _residue_major_matrix(ATT_GROUPS[2][1]))


ATT_STEP = 2
LOG2_E = 1.4426950408889634
LN_2 = 0.6931471805599453


def _attention_kernel(q_ref, k_ref, v_ref, o_ref, lse_ref, kp_ref, vp_ref, s_ref):
    n = pl.program_id(2)

    @pl.when(n == 0)
    def _():
        kp_ref[...] = jnp.zeros_like(kp_ref)
        vp_ref[...] = jnp.zeros_like(vp_ref)

    def block_rows(ref, b):
        if len(ref.shape) == 2:
            return (slice(b * ATT_BLOCK, (b + 1) * ATT_BLOCK),)
        per = ref.shape[0] // ATT_STEP
        return (slice(b * per, (b + 1) * per), slice(None))

    def load(ref, b, cols):
        x = ref[block_rows(ref, b) + (cols,)]
        return x.reshape(ATT_BLOCK, x.shape[-1])

    def store(ref, b, cols, value):
        idx = block_rows(ref, b) + (cols,)
        ref[idx] = value.reshape(ref[idx].shape)

    qi = lax.broadcasted_iota(jnp.int32, (ATT_BLOCK, 2 * ATT_BLOCK), 0)
    kj = lax.broadcasted_iota(jnp.int32, (ATT_BLOCK, 2 * ATT_BLOCK), 1)
    band = (kj >= qi) & (kj <= qi + ATT_BLOCK)
    first_band = band & ((n > 0) | (kj >= ATT_BLOCK))
    lane = lax.broadcasted_iota(jnp.int32, (ATT_BLOCK, LANES), 1)
    contract_last = (((1,), (1,)), ((), ()))
    head_cols = [slice(h * ATT_HEAD_DIM, (h + 1) * ATT_HEAD_DIM) for h in range(ATT_HEADS)]

    for b in range(ATT_STEP):
        for h, cols in enumerate(head_cols):
            q = load(q_ref, b, cols)
            k_prev = kp_ref[:, cols] if b == 0 else load(k_ref, b - 1, cols)
            s_ref[b * ATT_HEADS + h, :, :ATT_BLOCK] = lax.dot_general(
                q, k_prev, contract_last, preferred_element_type=F32)
            s_ref[b * ATT_HEADS + h, :, ATT_BLOCK:] = lax.dot_general(
                q, load(k_ref, b, cols), contract_last, preferred_element_type=F32)

    for b in range(ATT_STEP):
        valid = first_band if b == 0 else band
        max_tile = jnp.zeros((ATT_BLOCK, LANES), F32)
        sum_tile = jnp.ones((ATT_BLOCK, LANES), F32)
        for h, cols in enumerate(head_cols):
            s = jnp.where(valid, s_ref[b * ATT_HEADS + h], -jnp.inf)
            m = jnp.max(s, axis=-1, keepdims=True)
            p = jnp.exp2(s - m)
            denom = jnp.sum(p, axis=-1, keepdims=True)
            pb = p.astype(BF16)
            v_prev = vp_ref[:, cols] if b == 0 else load(v_ref, b - 1, cols)
            pv = (jnp.dot(pb[:, :ATT_BLOCK], v_prev, preferred_element_type=F32)
                  + jnp.dot(pb[:, ATT_BLOCK:], load(v_ref, b, cols), preferred_element_type=F32))
            store(o_ref, b, cols, (pv / denom).astype(o_ref.dtype))
            max_tile = jnp.where(lane == h, m, max_tile)
            sum_tile = jnp.where(lane == h, denom, sum_tile)
        store(lse_ref, b, slice(None), (max_tile + jnp.log2(sum_tile)) * LN_2)

    last = ATT_STEP - 1
    kp_ref[...] = load(k_ref, last, slice(None))
    vp_ref[...] = load(v_ref, last, slice(None))


def _attention_group(qkv, group, dilation):
    n_sub = ATT_TILE // dilation
    step = ATT_STEP * ATT_BLOCK
    n_steps = SEQ // dilation // step
    n_tiles = TOKENS // ATT_TILE
    tiles_per_batch = SEQ // ATT_TILE
    q_unit, k_unit, v_unit = group, N_ATT_GROUPS + group, 2 * N_ATT_GROUPS + group
    view = qkv.reshape(n_tiles, dilation, n_sub, ATT_IN_WIDTH)

    if n_sub >= step:
        per_tile = n_sub // step
        lead = (None, None, step)
        index = lambda b, c, n: (b * tiles_per_batch + n // per_tile, c, n % per_tile)
    else:
        span = step // n_sub
        lead = (span, None, n_sub)
        index = lambda b, c, n: (b * (tiles_per_batch // span) + n, c, 0)

    def spec(width, unit):
        return pl.BlockSpec(lead + (width,), lambda b, c, n: index(b, c, n) + (unit,))

    return pl.pallas_call(
        _attention_kernel,
        grid=(BATCH, dilation, n_steps),
        in_specs=[spec(ATT_WIDTH, q_unit), spec(ATT_WIDTH, k_unit), spec(ATT_WIDTH, v_unit)],
        out_specs=[spec(ATT_WIDTH, 0), spec(LANES, 0)],
        out_shape=[
            jax.ShapeDtypeStruct((n_tiles, dilation, n_sub, ATT_WIDTH), BF16),
            jax.ShapeDtypeStruct((n_tiles, dilation, n_sub, LANES), F32),
        ],
        scratch_shapes=[
            pltpu.VMEM((ATT_BLOCK, ATT_WIDTH), BF16),
            pltpu.VMEM((ATT_BLOCK, ATT_WIDTH), BF16),
            pltpu.VMEM((ATT_STEP * ATT_HEADS, ATT_BLOCK, 2 * ATT_BLOCK), F32),
        ],
        compiler_params=_params(("parallel", "parallel", "arbitrary")),
        name=f"attention_d{dilation}",
    )(view, view, view)


OUT_TM = 256


def _att_outproj_kernel(o0_ref, o1_ref, o2_ref, l0_ref, l1_ref, l2_ref, p1_ref, p2_ref, w_ref,
                        res_ref, out_ref, lse_ref, even_ref, odd_ref):
    s = pl.program_id(0)

    @pl.when(s == 0)
    def _():
        odd_ref[...] = jnp.zeros_like(odd_ref)

    def step(merged_prev_ref, merged_ref):
        out_ref[...] = res_ref[...] + jnp.dot(merged_prev_ref[...], w_ref[...],
                                              preferred_element_type=F32)
        for k, l_ref in enumerate((l1_ref, l2_ref)):
            d, n = l_ref.shape[0], l_ref.shape[1]
            for c in range(d):
                lse_ref[k, pl.ds(c, n, stride=d), :] = l_ref[c]
        l0, l1, l2 = l0_ref[0], lse_ref[0], lse_ref[1]
        top = jnp.maximum(jnp.maximum(l0, l1), l2)
        e0, e1, e2 = jnp.exp(l0 - top), jnp.exp(l1 - top), jnp.exp(l2 - top)
        total = e0 + e1 + e2
        w0, w1, w2 = e0 / total, e1 / total, e2 / total
        o1 = jnp.dot(p1_ref[...], o1_ref[...].reshape(OUT_TM, ATT_WIDTH),
                     preferred_element_type=F32)
        o2 = jnp.dot(p2_ref[...], o2_ref[...].reshape(OUT_TM, ATT_WIDTH),
                     preferred_element_type=F32)
        for h in range(ATT_HEADS):
            cols = slice(h * ATT_HEAD_DIM, (h + 1) * ATT_HEAD_DIM)
            merged = (w0[:, h:h + 1] * o0_ref[0, :, cols].astype(F32)
                      + w1[:, h:h + 1] * o1[:, cols]
                      + w2[:, h:h + 1] * o2[:, cols])
            merged_ref[:, cols] = merged.astype(BF16)

    @pl.when(s % 2 == 0)
    def _():
        step(odd_ref, even_ref)

    @pl.when(s % 2 == 1)
    def _():
        step(even_ref, odd_ref)


def _token_order_matrix(dilation):
    t = jnp.arange(OUT_TM)
    src = (t % dilation) * (OUT_TM // dilation) + t // dilation
    return (src[:, None] == t[None, :]).astype(BF16)


def _att_outproj(outs, lses, w, res):
    tm = OUT_TM
    per_tile = ATT_TILE // tm

    n_row_tiles = TOKENS // tm
    merge_tile = lambda s: jnp.minimum(s, n_row_tiles - 1)
    project_tile = lambda s: jnp.maximum(s - 1, 0)

    def spec(dilation, width):
        return pl.BlockSpec(
            (None, dilation, tm // dilation, width),
            lambda s: (merge_tile(s) // per_tile, 0, merge_tile(s) % per_tile, 0))

    dils = [d for _, d in ATT_GROUPS]
    const = lambda shape: pl.BlockSpec(shape, lambda s: (0, 0))
    return pl.pallas_call(
        _att_outproj_kernel,
        grid=(n_row_tiles + 1,),
        in_specs=[spec(d, ATT_WIDTH) for d in dils] + [spec(d, LANES) for d in dils] + [
            const((tm, tm)), const((tm, tm)),
            pl.BlockSpec((None, ATT_WIDTH, D_MODEL), lambda s: (0, 0, 0)),
            pl.BlockSpec((tm, D_MODEL), lambda s: (project_tile(s), 0))],
        out_specs=pl.BlockSpec((tm, D_MODEL), lambda s: (project_tile(s), 0)),
        out_shape=jax.ShapeDtypeStruct((TOKENS, D_MODEL), F32),
        scratch_shapes=[pltpu.VMEM((N_ATT_GROUPS - 1, tm, LANES), F32),
                        pltpu.VMEM((tm, ATT_WIDTH), BF16),
                        pltpu.VMEM((tm, ATT_WIDTH), BF16)],
        compiler_params=_params(("arbitrary",)),
        name="att_outproj",
    )(*outs, *lses, _token_order_matrix(dils[1]), _token_order_matrix(dils[2]), w, res)


def _retention_tables():
    pos = jnp.arange(SEQ, dtype=F32)
    inv_freq = 1.0 / (RET_ROT_BASE ** jnp.linspace(0.0, 1.0, RET_DK // 2, dtype=F32))
    ang = pos[:, None] * inv_freq[None, :]
    log_gamma = jnp.log(1.0 - 2.0 ** (-5.0 - jnp.arange(RET_HEADS, dtype=F32)))
    idx = jnp.arange(RET_BLOCK, dtype=F32)
    diff = idx[:, None] - idx[None, :]
    inner_decay = jnp.where(diff >= 0,
                            jnp.exp(log_gamma[:, None, None] * jnp.maximum(diff, 0.0)), 0.0)
    query_decay = jnp.exp(log_gamma[:, None] * (idx[None, :] + 1.0))[:, :, None]
    key_decay = jnp.exp(log_gamma[:, None] * (RET_BLOCK - 1.0 - idx[None, :]))[:, :, None]
    chunk_decay = jnp.broadcast_to(jnp.exp(log_gamma * RET_BLOCK)[:, None, None],
                                   (RET_HEADS, 1, RET_DV))
    return jnp.cos(ang), jnp.sin(ang), inner_decay, query_decay, key_decay, chunk_decay


def _attention_tables():
    pos = jnp.arange(SEQ, dtype=F32)
    pos = jnp.stack([
        pos.reshape(SEQ // ATT_TILE, ATT_TILE // d, d).swapaxes(1, 2).reshape(SEQ)
        for _, d in ATT_GROUPS])
    inv_freq = ROPE_THETA ** (-jnp.arange(0, ROPE_DIM, 2, dtype=F32) / ROPE_DIM)
    ang = pos[:, :, None] * inv_freq[None, None, :]
    cos, sin = jnp.cos(ang), jnp.sin(ang)
    zeros = jnp.zeros((N_ATT_GROUPS, SEQ, ATT_HEAD_DIM - ROPE_DIM), F32)
    cos_full = jnp.concatenate([cos, cos, jnp.ones_like(zeros)], axis=2)
    sin_full = jnp.concatenate([sin, sin, zeros], axis=2)
    return cos_full, sin_full


def kernel(x, norm_mix_gain, norm_mlp_gain, ret_w_in, ret_w_out, att_w_in, att_q_gain,
           att_k_gain, att_w_out, mlp_w_in, mlp_w_out):
    h = x.reshape(TOKENS, D_MODEL)

    cos, sin, inner_decay, query_decay, key_decay, chunk_decay = _retention_tables()
    proj, ret_wo, mlp0_w1, mlp0_w2 = _ret_inproj(
        h, norm_mix_gain[0:1], ret_w_in.astype(BF16), cos, sin,
        [(ret_w_out, 0), (mlp_w_in, 0), (mlp_w_out, 0)])
    y = _retention(proj, inner_decay, query_decay, key_decay, chunk_decay)
    h = _outproj(y, ret_wo, h)
    h, att_wi, att_wo, mlp1_w1, mlp1_w2 = _mlp(
        h, norm_mlp_gain[0:1], mlp0_w1, mlp0_w2,
        [(att_w_in, 0), (att_w_out, 0), (mlp_w_in, 1), (mlp_w_out, 1)])

    cos_full, sin_full = _attention_tables()
    qkv = _att_inproj(h, norm_mix_gain[1:2], att_wi, att_q_gain[0:1], att_k_gain[0:1],
                      cos_full, sin_full)
    outs, lses = [], []
    for g, (_, dilation) in enumerate(ATT_GROUPS):
        o, l = _attention_group(qkv, g, dilation)
        outs.append(o)
        lses.append(l)
    h = _att_outproj(outs, lses, att_wo, h)
    (h,) = _mlp(h, norm_mlp_gain[1:2], mlp1_w1, mlp1_w2)
    return h.reshape(BATCH, SEQ, D_MODEL)
```

```python
import functools

import jax
import jax.numpy as jnp
from jax import lax
from jax.experimental import pallas as pl
from jax.experimental.pallas import tpu as pltpu

F32 = jnp.float32
BF16 = jnp.bfloat16

D_MODEL = 2048
BATCH = 2
SEQ = 8192
TOKENS = BATCH * SEQ
EPS = 1e-6

RET_HEADS = 8
RET_DK = 256
RET_DV = 512
RET_QK_WIDTH = RET_HEADS * RET_DK
RET_V_WIDTH = RET_HEADS * RET_DV
RET_IN_WIDTH = 2 * RET_QK_WIDTH + 2 * RET_V_WIDTH
RET_CHUNK = 128
RET_ROT_BASE = 10000.0

ATT_GROUPS = ((128, 1), (512, 4), (2048, 16))
N_ATT_GROUPS = 3
ATT_HEAD_DIM = 128
ATT_HEADS = 16
ATT_WIDTH = ATT_HEADS * ATT_HEAD_DIM
ATT_IN_WIDTH = 3 * N_ATT_GROUPS * ATT_WIDTH
ATT_BLOCK = 128
ROPE_DIM = 32
ROPE_HALF = ROPE_DIM // 2
ROPE_THETA = 500000.0

D_FF = 4 * D_MODEL

LANES = 128
VMEM_LIMIT = 56 * 1024 * 1024


def _params(semantics):
    return pltpu.CompilerParams(dimension_semantics=semantics, vmem_limit_bytes=VMEM_LIMIT)


def _rms_norm_bf16(x, gain):
    ms = jnp.mean(x * x, axis=-1, keepdims=True)
    return (x * lax.rsqrt(ms + EPS) * gain).astype(BF16)


SIDE_CAST_BLOCKS = 128


def _side_cast(weights, step_of):
    operands, in_specs, out_specs, out_shapes = [], [], [], []
    for arr, layer in weights:
        _, rows, cols = arr.shape
        block = (None, rows // SIDE_CAST_BLOCKS, cols)
        blk = lambda *ids: jnp.minimum(step_of(*ids), SIDE_CAST_BLOCKS - 1)
        operands.append(arr)
        in_specs.append(pl.BlockSpec(block, lambda *ids, layer=layer, blk=blk: (layer, blk(*ids), 0)))
        out_specs.append(pl.BlockSpec(block, lambda *ids, blk=blk: (0, blk(*ids), 0)))
        out_shapes.append(jax.ShapeDtypeStruct((1, rows, cols), BF16))
    return operands, in_specs, out_specs, out_shapes


def _run_side_casts(src_refs, dst_refs):
    for src, dst in zip(src_refs, dst_refs):
        dst[...] = src[...].astype(BF16)


RET_ROW_PARTS = 2


def _ret_inproj_kernel(h_ref, gain_ref, w_ref, cos_ref, sin_ref, *rest,
                       n_side, n_q_tiles, n_rot_tiles, n_plain_tiles, heads_per_tile):
    side_in, o_ref, side_out, hn_ref = (rest[:n_side], rest[n_side],
                                        rest[n_side + 1:2 * n_side + 1], rest[2 * n_side + 1])
    j = pl.program_id(1)

    @pl.when(j == 0)
    def _():
        hn_ref[...] = _rms_norm_bf16(h_ref[...], gain_ref[...])

    row_part = o_ref.shape[0] // RET_ROW_PARTS
    units = [(hh, r) for hh in range(heads_per_tile) for r in range(RET_ROW_PARTS)]

    def unit_dot(hh, r):
        return jnp.dot(hn_ref[r * row_part:(r + 1) * row_part, :],
                       w_ref[:, hh * RET_DK:(hh + 1) * RET_DK], preferred_element_type=F32)

    @pl.when((j >= n_rot_tiles) & (j < n_plain_tiles))
    def _():
        _run_side_casts(side_in, side_out)
        for hh, r in units:
            o_ref[r * row_part:(r + 1) * row_part, hh * RET_DK:(hh + 1) * RET_DK] = (
                unit_dot(hh, r).astype(o_ref.dtype))

    @pl.when(j >= n_plain_tiles)
    def _():
        _run_side_casts(side_in, side_out)
        for hh, r in units:
            gate = unit_dot(hh, r)
            o_ref[r * row_part:(r + 1) * row_part, hh * RET_DK:(hh + 1) * RET_DK] = (
                gate * jax.nn.sigmoid(gate)).astype(o_ref.dtype)

    @pl.when(j < n_rot_tiles)
    def _():
        _run_side_casts(side_in, side_out)
        scale = jnp.where(j >= n_q_tiles, RET_DK ** -0.5, 1.0).astype(F32)
        half = RET_DK // 2
        for hh, r in units:
            rows = slice(r * row_part, (r + 1) * row_part)
            c = cos_ref[rows, :] * scale
            s = sin_ref[rows, :] * scale
            acc = unit_dot(hh, r)
            lo = hh * RET_DK
            x1 = acc[:, :half]
            x2 = acc[:, half:]
            o_ref[rows, lo:lo + half] = (x1 * c - x2 * s).astype(o_ref.dtype)
            o_ref[rows, lo + half:lo + RET_DK] = (x2 * c + x1 * s).astype(o_ref.dtype)


def _ret_inproj(h, gain, w, cos, sin, side_weights, *, tm=1024, tn=1024):
    n_pos_blocks = SEQ // tm
    n_col_tiles = RET_IN_WIDTH // tn
    side_ops, side_in, side_out, side_shapes = _side_cast(
        side_weights, lambda i, j: i * n_col_tiles + j)
    kern = functools.partial(
        _ret_inproj_kernel, n_side=len(side_ops), n_q_tiles=RET_QK_WIDTH // tn,
        n_rot_tiles=2 * RET_QK_WIDTH // tn,
        n_plain_tiles=(2 * RET_QK_WIDTH + RET_V_WIDTH) // tn, heads_per_tile=tn // RET_DK)
    return pl.pallas_call(
        kern,
        grid=(TOKENS // tm, n_col_tiles),
        in_specs=[
            pl.BlockSpec((tm, D_MODEL), lambda i, j: (i, 0)),
            pl.BlockSpec((1, D_MODEL), lambda i, j: (0, 0)),
            pl.BlockSpec((None, D_MODEL, tn), lambda i, j: (0, 0, j)),
            pl.BlockSpec((tm, RET_DK // 2), lambda i, j: (i % n_pos_blocks, 0)),
            pl.BlockSpec((tm, RET_DK // 2), lambda i, j: (i % n_pos_blocks, 0)),
        ] + side_in,
        out_specs=[pl.BlockSpec((tm, tn), lambda i, j: (i, j))] + side_out,
        out_shape=[jax.ShapeDtypeStruct((TOKENS, RET_IN_WIDTH), BF16)] + side_shapes,
        scratch_shapes=[pltpu.VMEM((tm, D_MODEL), BF16)],
        compiler_params=_params(("arbitrary", "arbitrary")),
        name="ret_inproj",
    )(h, gain, w, cos, sin, *side_ops)


RET_BLOCK = 256


def _retention_kernel(q_ref, k_ref, v_ref, g_ref, dec_ref, qd_ref, kd_ref, cd_ref, o_ref,
                      state_ref, acc_ref, upd_ref, *, n_chunks):
    @pl.when(pl.program_id(2) == 0)
    def _():
        state_ref[...] = jnp.zeros_like(state_ref)

    dec = dec_ref[...]
    qd = qd_ref[...]
    kd = kd_ref[...]
    cd = cd_ref[...]
    chunk = lambda c: slice(c * RET_BLOCK, (c + 1) * RET_BLOCK)

    for c in range(n_chunks):
        qc, kc, vc = q_ref[chunk(c), :], k_ref[chunk(c), :], v_ref[chunk(c), :]
        scores = lax.dot_general(qc, kc, (((1,), (1,)), ((), ())),
                                 preferred_element_type=F32) * dec
        acc_ref[chunk(c), :] = jnp.dot(scores.astype(BF16), vc, preferred_element_type=F32)
        k_dec = (kc.astype(F32) * kd).astype(BF16)
        upd_ref[c] = lax.dot_general(k_dec, vc, (((0,), (0,)), ((), ())),
                                     preferred_element_type=F32)

    for c in range(n_chunks):
        state = state_ref[...]
        cross = jnp.dot(q_ref[chunk(c), :], state.astype(BF16), preferred_element_type=F32) * qd
        state_ref[...] = state * cd + upd_ref[c]
        out = acc_ref[chunk(c), :] + cross
        ms = jnp.mean(out * out, axis=-1, keepdims=True)
        y = out * lax.rsqrt(ms + EPS) * g_ref[chunk(c), :].astype(F32)
        o_ref[chunk(c), :] = y.astype(o_ref.dtype)


def _retention(proj, dec, qd, kd, cd, *, rows=1024):
    n_row_blocks = SEQ // rows
    n_chunks = rows // RET_BLOCK
    k_off = RET_QK_WIDTH // RET_DK
    v_off = 2 * RET_QK_WIDTH // RET_DV
    g_off = v_off + RET_V_WIDTH // RET_DV
    row = lambda b, h, r: b * n_row_blocks + r
    kern = functools.partial(_retention_kernel, n_chunks=n_chunks)
    return pl.pallas_call(
        kern,
        grid=(BATCH, RET_HEADS, n_row_blocks),
        in_specs=[
            pl.BlockSpec((rows, RET_DK), lambda b, h, r: (row(b, h, r), h)),
            pl.BlockSpec((rows, RET_DK), lambda b, h, r: (row(b, h, r), k_off + h)),
            pl.BlockSpec((rows, RET_DV), lambda b, h, r: (row(b, h, r), v_off + h)),
            pl.BlockSpec((rows, RET_DV), lambda b, h, r: (row(b, h, r), g_off + h)),
            pl.BlockSpec((None, RET_BLOCK, RET_BLOCK), lambda b, h, r: (h, 0, 0)),
            pl.BlockSpec((None, RET_BLOCK, 1), lambda b, h, r: (h, 0, 0)),
            pl.BlockSpec((None, RET_BLOCK, 1), lambda b, h, r: (h, 0, 0)),
            pl.BlockSpec((None, 1, RET_DV), lambda b, h, r: (h, 0, 0)),
        ],
        out_specs=pl.BlockSpec((rows, RET_DV), lambda b, h, r: (row(b, h, r), h)),
        out_shape=jax.ShapeDtypeStruct((TOKENS, RET_V_WIDTH), BF16),
        scratch_shapes=[pltpu.VMEM((RET_DK, RET_DV), F32),
                        pltpu.VMEM((rows, RET_DV), F32),
                        pltpu.VMEM((n_chunks, RET_DK, RET_DV), F32)],
        compiler_params=_params(("parallel", "parallel", "arbitrary")),
        name="retention",
    )(proj, proj, proj, proj, dec, qd, kd, cd)


def _outproj_kernel(y_ref, w_ref, res_ref, o_ref):
    o_ref[...] = res_ref[...] + jnp.dot(y_ref[...], w_ref[...], preferred_element_type=F32)


def _outproj(y, w, res, *, tm=512, tn=1024):
    k = y.shape[1]
    return pl.pallas_call(
        _outproj_kernel,
        grid=(D_MODEL // tn, TOKENS // tm),
        in_specs=[
            pl.BlockSpec((tm, k), lambda j, i: (i, 0)),
            pl.BlockSpec((None, k, tn), lambda j, i: (0, 0, j)),
            pl.BlockSpec((tm, tn), lambda j, i: (i, j)),
        ],
        out_specs=pl.BlockSpec((tm, tn), lambda j, i: (i, j)),
        out_shape=jax.ShapeDtypeStruct((TOKENS, D_MODEL), F32),
        compiler_params=_params(("parallel", "parallel")),
        name="outproj",
    )(y, w, res)


def _mlp_kernel(h_ref, gain_ref, w1_ref, w2_ref, *rest, n_side):
    side_in, o_ref, side_out, hn_ref = (rest[:n_side], rest[n_side],
                                        rest[n_side + 1:2 * n_side + 1], rest[2 * n_side + 1])
    @pl.when(pl.program_id(1) == 0)
    def _():
        h = h_ref[...]
        hn_ref[...] = _rms_norm_bf16(h, gain_ref[...])
        o_ref[...] = h

    _run_side_casts(side_in, side_out)
    a = jnp.dot(hn_ref[...], w1_ref[...], preferred_element_type=F32)
    act = jnp.square(jnp.maximum(a, 0.0)).astype(BF16)
    o_ref[...] += jnp.dot(act, w2_ref[...], preferred_element_type=F32)


def _mlp(h, gain, w1, w2, side_weights=(), *, tm=512, tf=1024):
    n_ff_tiles = D_FF // tf
    side_ops, side_in, side_out, side_shapes = _side_cast(
        side_weights, lambda i, f: i * n_ff_tiles + f)
    return pl.pallas_call(
        functools.partial(_mlp_kernel, n_side=len(side_ops)),
        grid=(TOKENS // tm, n_ff_tiles),
        in_specs=[
            pl.BlockSpec((tm, D_MODEL), lambda i, f: (i, 0)),
            pl.BlockSpec((1, D_MODEL), lambda i, f: (0, 0)),
            pl.BlockSpec((None, D_MODEL, tf), lambda i, f: (0, 0, f)),
            pl.BlockSpec((None, tf, D_MODEL), lambda i, f: (0, f, 0)),
        ] + side_in,
        out_specs=[pl.BlockSpec((tm, D_MODEL), lambda i, f: (i, 0))] + side_out,
        out_shape=[jax.ShapeDtypeStruct((TOKENS, D_MODEL), F32)] + side_shapes,
        scratch_shapes=[pltpu.VMEM((tm, D_MODEL), BF16)],
        compiler_params=_params(("arbitrary", "arbitrary")),
        name="mlp",
    )(h, gain, w1, w2, *side_ops)


ATT_TILE = 256
ATT_SUB = 256
ATT_ROW_PARTS = 2


def _residue_major_matrix(dilation):
    r = jnp.arange(ATT_TILE)
    n = ATT_TILE // dilation
    src = (r % n) * dilation + r // n
    return (src[:, None] == r[None, :]).astype(BF16)


def _head_aux_matrix():
    i = jnp.arange(ATT_SUB)
    row, col = i[:, None], i[None, :]
    same_head = (row // ATT_HEAD_DIM) == (col // ATT_HEAD_DIM)
    r, c = row % ATT_HEAD_DIM, col % ATT_HEAD_DIM
    minus = same_head & (c < ROPE_HALF) & (r == c + ROPE_HALF)
    plus = same_head & (r < ROPE_HALF) & (c == r + ROPE_HALF)
    return (plus.astype(F32) - minus.astype(F32)).astype(BF16)


def _att_inproj_kernel(h_ref, gain_ref, w_ref, qg_ref, kg_ref, cos_ref, sin_ref, aux_ref,
                       p1_ref, p2_ref, o_ref, hn_ref, *, tiles_per_unit, n_q_tiles, n_qk_tiles):
    j = pl.program_id(1)
    tm, tn = o_ref.shape

    @pl.when(j == 0)
    def _():
        hn_ref[0] = _rms_norm_bf16(h_ref[...], gain_ref[...])
        for g, p_ref in ((1, p1_ref), (2, p2_ref)):
            for t in range(tm // ATT_TILE):
                rows = slice(t * ATT_TILE, (t + 1) * ATT_TILE)
                hn_ref[g, rows, :] = jnp.dot(p_ref[...], hn_ref[0, rows, :],
                                             preferred_element_type=F32).astype(BF16)

    g = (j // tiles_per_unit) % N_ATT_GROUPS

    row_part = tm // ATT_ROW_PARTS
    units = [(k, r) for k in range(tn // ATT_SUB) for r in range(ATT_ROW_PARTS)]

    def unit_dot(k, r):
        return jnp.dot(hn_ref[g, r * row_part:(r + 1) * row_part, :],
                       w_ref[:, k * ATT_SUB:(k + 1) * ATT_SUB], preferred_element_type=F32)

    @pl.when(j >= n_qk_tiles)
    def _():
        for k, r in units:
            o_ref[r * row_part:(r + 1) * row_part, k * ATT_SUB:(k + 1) * ATT_SUB] = (
                unit_dot(k, r).astype(o_ref.dtype))

    @pl.when(j < n_qk_tiles)
    def _():
        is_q = j < n_q_tiles
        head_gain = jnp.where(is_q, qg_ref[...] * (ATT_HEAD_DIM ** -0.5 * LOG2_E), kg_ref[...])

        def epilogue(k, r, acc):
            rows = slice(r * row_part, (r + 1) * row_part)
            c = cos_ref[rows, :]
            s = sin_ref[rows, :]
            xg = acc * jnp.concatenate([head_gain, head_gain], axis=1)
            partner = jnp.dot(xg.astype(BF16), aux_ref[...], preferred_element_type=F32)
            for hh in range(ATT_SUB // ATT_HEAD_DIM):
                cols = slice(hh * ATT_HEAD_DIM, (hh + 1) * ATT_HEAD_DIM)
                x = acc[:, cols]
                inv = lax.rsqrt(jnp.mean(x * x, axis=-1, keepdims=True) + EPS)
                rot = (xg[:, cols] * c + partner[:, cols] * s) * inv
                lo = k * ATT_SUB + hh * ATT_HEAD_DIM
                o_ref[rows, lo:lo + ATT_HEAD_DIM] = rot.astype(o_ref.dtype)

        acc = unit_dot(*units[0])
        for n, (k, r) in enumerate(units):
            nxt = unit_dot(*units[n + 1]) if n + 1 < len(units) else None
            epilogue(k, r, acc)
            acc = nxt


def _att_inproj(h, gain, w, q_gain, k_gain, cos, sin, *, tm=1024, tn=1024):
    n_pos_blocks = SEQ // tm
    group_width = N_ATT_GROUPS * ATT_WIDTH
    tiles_per_unit = ATT_WIDTH // tn
    kern = functools.partial(
        _att_inproj_kernel, tiles_per_unit=tiles_per_unit, n_q_tiles=group_width // tn,
        n_qk_tiles=2 * group_width // tn)
    pos_spec = pl.BlockSpec(
        (None, tm, ATT_HEAD_DIM),
        lambda i, j: ((j // tiles_per_unit) % N_ATT_GROUPS, i % n_pos_blocks, 0))
    return pl.pallas_call(
        kern,
        grid=(TOKENS // tm, ATT_IN_WIDTH // tn),
        in_specs=[
            pl.BlockSpec((tm, D_MODEL), lambda i, j: (i, 0)),
            pl.BlockSpec((1, D_MODEL), lambda i, j: (0, 0)),
            pl.BlockSpec((None, D_MODEL, tn), lambda i, j: (0, 0, j)),
            pl.BlockSpec((1, ATT_HEAD_DIM), lambda i, j: (0, 0)),
            pl.BlockSpec((1, ATT_HEAD_DIM), lambda i, j: (0, 0)),
            pos_spec, pos_spec,
            pl.BlockSpec((2 * ATT_HEAD_DIM, 2 * ATT_HEAD_DIM), lambda i, j: (0, 0)),
            pl.BlockSpec((ATT_TILE, ATT_TILE), lambda i, j: (0, 0)),
            pl.BlockSpec((ATT_TILE, ATT_TILE), lambda i, j: (0, 0)),
        ],
        out_specs=pl.BlockSpec((tm, tn), lambda i, j: (i, j)),
        out_shape=jax.ShapeDtypeStruct((TOKENS, ATT_IN_WIDTH), BF16),
        scratch_shapes=[pltpu.VMEM((N_ATT_GROUPS, tm, D_MODEL), BF16)],
        compiler_params=_params(("parallel", "arbitrary")),
        name="att_inproj",
    )(h, gain, w, q_gain, k_gain, cos, sin, _head_aux_matrix(),
      _residue_major_matrix(ATT_GROUPS[1][1]), _residue_major_matrix(ATT_GROUPS[2][1]))


ATT_STEP = 4
LOG2_E = 1.4426950408889634
LN_2 = 0.6931471805599453


def _attention_kernel(q_ref, k_ref, v_ref, o_ref, lse_ref, kp_ref, vp_ref, s_ref):
    n = pl.program_id(2)

    @pl.when(n == 0)
    def _():
        kp_ref[...] = jnp.zeros_like(kp_ref)
        vp_ref[...] = jnp.zeros_like(vp_ref)

    def block_rows(ref, b):
        if len(ref.shape) == 2:
            return (slice(b * ATT_BLOCK, (b + 1) * ATT_BLOCK),), (ATT_BLOCK,)
        n_sub = ref.shape[1]
        if n_sub >= ATT_BLOCK:
            tile, off = divmod(b * ATT_BLOCK, n_sub)
            return (tile, slice(off, off + ATT_BLOCK)), (ATT_BLOCK,)
        per = ATT_BLOCK // n_sub
        return (slice(b * per, (b + 1) * per), slice(None)), (per, n_sub)

    def load(ref, b, cols):
        idx, _ = block_rows(ref, b)
        x = ref[idx + (cols,)]
        return x.reshape(ATT_BLOCK, x.shape[-1])

    def store(ref, b, cols, value):
        idx, lead = block_rows(ref, b)
        ref[idx + (cols,)] = value.reshape(lead + (value.shape[-1],))

    qi = lax.broadcasted_iota(jnp.int32, (ATT_BLOCK, 2 * ATT_BLOCK), 0)
    kj = lax.broadcasted_iota(jnp.int32, (ATT_BLOCK, 2 * ATT_BLOCK), 1)
    band = (kj >= qi) & (kj <= qi + ATT_BLOCK)
    first_band = band & ((n > 0) | (kj >= ATT_BLOCK))
    lane = lax.broadcasted_iota(jnp.int32, (ATT_BLOCK, LANES), 1)
    contract_last = (((1,), (1,)), ((), ()))
    head_cols = [slice(h * ATT_HEAD_DIM, (h + 1) * ATT_HEAD_DIM) for h in range(ATT_HEADS)]

    for b in range(ATT_STEP):
        for h, cols in enumerate(head_cols):
            q = load(q_ref, b, cols)
            k_prev = kp_ref[:, cols] if b == 0 else load(k_ref, b - 1, cols)
            s_ref[b * ATT_HEADS + h, :, :ATT_BLOCK] = lax.dot_general(
                q, k_prev, contract_last, preferred_element_type=F32)
            s_ref[b * ATT_HEADS + h, :, ATT_BLOCK:] = lax.dot_general(
                q, load(k_ref, b, cols), contract_last, preferred_element_type=F32)

    for b in range(ATT_STEP):
        valid = first_band if b == 0 else band
        max_tile = jnp.zeros((ATT_BLOCK, LANES), F32)
        sum_tile = jnp.ones((ATT_BLOCK, LANES), F32)
        for h, cols in enumerate(head_cols):
            s = jnp.where(valid, s_ref[b * ATT_HEADS + h], -jnp.inf)
            m = jnp.max(s, axis=-1, keepdims=True)
            p = jnp.exp2(s - m)
            denom = jnp.sum(p, axis=-1, keepdims=True)
            pb = p.astype(BF16)
            v_prev = vp_ref[:, cols] if b == 0 else load(v_ref, b - 1, cols)
            pv = (jnp.dot(pb[:, :ATT_BLOCK], v_prev, preferred_element_type=F32)
                  + jnp.dot(pb[:, ATT_BLOCK:], load(v_ref, b, cols), preferred_element_type=F32))
            store(o_ref, b, cols, (pv / denom).astype(o_ref.dtype))
            max_tile = jnp.where(lane == h, m, max_tile)
            sum_tile = jnp.where(lane == h, denom, sum_tile)
        store(lse_ref, b, slice(None), (max_tile + jnp.log2(sum_tile)) * LN_2)

    last = ATT_STEP - 1
    kp_ref[...] = load(k_ref, last, slice(None))
    vp_ref[...] = load(v_ref, last, slice(None))


def _attention_group(qkv, group, dilation):
    n_sub = ATT_TILE // dilation
    step = ATT_STEP * ATT_BLOCK
    n_steps = SEQ // dilation // step
    n_tiles = TOKENS // ATT_TILE
    tiles_per_batch = SEQ // ATT_TILE
    q_unit, k_unit, v_unit = group, N_ATT_GROUPS + group, 2 * N_ATT_GROUPS + group
    view = qkv.reshape(n_tiles, dilation, n_sub, ATT_IN_WIDTH)

    if n_sub >= step:
        per_tile = n_sub // step
        lead = (None, None, step)
        index = lambda b, c, n: (b * tiles_per_batch + n // per_tile, c, n % per_tile)
    else:
        span = step // n_sub
        lead = (span, None, n_sub)
        index = lambda b, c, n: (b * (tiles_per_batch // span) + n, c, 0)

    def spec(width, unit):
        return pl.BlockSpec(lead + (width,), lambda b, c, n: index(b, c, n) + (unit,))

    return pl.pallas_call(
        _attention_kernel,
        grid=(BATCH, dilation, n_steps),
        in_specs=[spec(ATT_WIDTH, q_unit), spec(ATT_WIDTH, k_unit), spec(ATT_WIDTH, v_unit)],
        out_specs=[spec(ATT_WIDTH, 0), spec(LANES, 0)],
        out_shape=[
            jax.ShapeDtypeStruct((n_tiles, dilation, n_sub, ATT_WIDTH), BF16),
            jax.ShapeDtypeStruct((n_tiles, dilation, n_sub, LANES), F32),
        ],
        scratch_shapes=[
            pltpu.VMEM((ATT_BLOCK, ATT_WIDTH), BF16),
            pltpu.VMEM((ATT_BLOCK, ATT_WIDTH), BF16),
            pltpu.VMEM((ATT_STEP * ATT_HEADS, ATT_BLOCK, 2 * ATT_BLOCK), F32),
        ],
        compiler_params=_params(("parallel", "parallel", "arbitrary")),
        name=f"attention_d{dilation}",
    )(view, view, view)


OUT_TM = 256


def _att_outproj_kernel(o0_ref, o1_ref, o2_ref, l0_ref, l1_ref, l2_ref, p1_ref, p2_ref, w_ref,
                        res_ref, out_ref, lse_ref, even_ref, odd_ref):
    s = pl.program_id(0)

    @pl.when(s == 0)
    def _():
        odd_ref[...] = jnp.zeros_like(odd_ref)

    def step(merged_prev_ref, merged_ref):
        out_ref[...] = res_ref[...] + jnp.dot(merged_prev_ref[...], w_ref[...],
                                              preferred_element_type=F32)
        for k, l_ref in enumerate((l1_ref, l2_ref)):
            d, n = l_ref.shape[0], l_ref.shape[1]
            for c in range(d):
                lse_ref[k, pl.ds(c, n, stride=d), :] = l_ref[c]
        l0, l1, l2 = l0_ref[0], lse_ref[0], lse_ref[1]
        top = jnp.maximum(jnp.maximum(l0, l1), l2)
        e0, e1, e2 = jnp.exp(l0 - top), jnp.exp(l1 - top), jnp.exp(l2 - top)
        total = e0 + e1 + e2
        w0, w1, w2 = e0 / total, e1 / total, e2 / total
        o1 = jnp.dot(p1_ref[...], o1_ref[...].reshape(OUT_TM, ATT_WIDTH),
                     preferred_element_type=F32)
        o2 = jnp.dot(p2_ref[...], o2_ref[...].reshape(OUT_TM, ATT_WIDTH),
                     preferred_element_type=F32)
        for h in range(ATT_HEADS):
            cols = slice(h * ATT_HEAD_DIM, (h + 1) * ATT_HEAD_DIM)
            merged = (w0[:, h:h + 1] * o0_ref[0, :, cols].astype(F32)
                      + w1[:, h:h + 1] * o1[:, cols]
                      + w2[:, h:h + 1] * o2[:, cols])
            merged_ref[:, cols] = merged.astype(BF16)

    @pl.when(s % 2 == 0)
    def _():
        step(odd_ref, even_ref)

    @pl.when(s % 2 == 1)
    def _():
        step(even_ref, odd_ref)


def _token_order_matrix(dilation):
    t = jnp.arange(OUT_TM)
    src = (t % dilation) * (OUT_TM // dilation) + t // dilation
    return (src[:, None] == t[None, :]).astype(BF16)


def _att_outproj(outs, lses, w, res):
    tm = OUT_TM
    per_tile = ATT_TILE // tm

    n_row_tiles = TOKENS // tm
    merge_tile = lambda s: jnp.minimum(s, n_row_tiles - 1)
    project_tile = lambda s: jnp.maximum(s - 1, 0)

    def spec(dilation, width):
        return pl.BlockSpec(
            (None, dilation, tm // dilation, width),
            lambda s: (merge_tile(s) // per_tile, 0, merge_tile(s) % per_tile, 0))

    dils = [d for _, d in ATT_GROUPS]
    const = lambda shape: pl.BlockSpec(shape, lambda s: (0, 0))
    return pl.pallas_call(
        _att_outproj_kernel,
        grid=(n_row_tiles + 1,),
        in_specs=[spec(d, ATT_WIDTH) for d in dils] + [spec(d, LANES) for d in dils] + [
            const((tm, tm)), const((tm, tm)),
            pl.BlockSpec((None, ATT_WIDTH, D_MODEL), lambda s: (0, 0, 0)),
            pl.BlockSpec((tm, D_MODEL), lambda s: (project_tile(s), 0))],
        out_specs=pl.BlockSpec((tm, D_MODEL), lambda s: (project_tile(s), 0)),
        out_shape=jax.ShapeDtypeStruct((TOKENS, D_MODEL), F32),
        scratch_shapes=[pltpu.VMEM((N_ATT_GROUPS - 1, tm, LANES), F32),
                        pltpu.VMEM((tm, ATT_WIDTH), BF16),
                        pltpu.VMEM((tm, ATT_WIDTH), BF16)],
        compiler_params=_params(("arbitrary",)),
        name="att_outproj",
    )(*outs, *lses, _token_order_matrix(dils[1]), _token_order_matrix(dils[2]), w, res)


def _cos_sin(pos_lo, inv_freq):
    pos_hi = jnp.arange(SEQ // ATT_TILE, dtype=F32) * ATT_TILE
    ang_hi = pos_hi[:, None] * inv_freq
    ang_lo = pos_lo[..., None] * inv_freq
    ch, sh, cl, sl = lax.optimization_barrier(
        (jnp.cos(ang_hi), jnp.sin(ang_hi), jnp.cos(ang_lo), jnp.sin(ang_lo)))
    ch, sh = ch[:, None, :], sh[:, None, :]
    cl, sl = cl[..., None, :, :], sl[..., None, :, :]
    shape = pos_lo.shape[:-1] + (SEQ, inv_freq.shape[0])
    return (ch * cl - sh * sl).reshape(shape), (sh * cl + ch * sl).reshape(shape)


def _retention_tables():
    inv_freq = 1.0 / (RET_ROT_BASE ** jnp.linspace(0.0, 1.0, RET_DK // 2, dtype=F32))
    cos, sin = _cos_sin(jnp.arange(ATT_TILE, dtype=F32), inv_freq)
    log_gamma = jnp.log(1.0 - 2.0 ** (-5.0 - jnp.arange(RET_HEADS, dtype=F32)))
    idx = jnp.arange(RET_BLOCK, dtype=F32)
    diff = idx[:, None] - idx[None, :]
    inner_decay = jnp.where(diff >= 0,
                            jnp.exp(log_gamma[:, None, None] * jnp.maximum(diff, 0.0)), 0.0)
    query_decay = jnp.exp(log_gamma[:, None] * (idx[None, :] + 1.0))[:, :, None]
    key_decay = jnp.exp(log_gamma[:, None] * (RET_BLOCK - 1.0 - idx[None, :]))[:, :, None]
    chunk_decay = jnp.broadcast_to(jnp.exp(log_gamma * RET_BLOCK)[:, None, None],
                                   (RET_HEADS, 1, RET_DV))
    return cos, sin, inner_decay, query_decay, key_decay, chunk_decay


def _attention_tables():
    pos_lo = jnp.arange(ATT_TILE, dtype=F32)
    pos_lo = jnp.stack([pos_lo.reshape(ATT_TILE // d, d).T.reshape(ATT_TILE)
                        for _, d in ATT_GROUPS])
    inv_freq = ROPE_THETA ** (-jnp.arange(0, ROPE_DIM, 2, dtype=F32) / ROPE_DIM)
    cos, sin = _cos_sin(pos_lo, inv_freq)
    zeros = jnp.zeros((N_ATT_GROUPS, SEQ, ATT_HEAD_DIM - ROPE_DIM), F32)
    cos_full = jnp.concatenate([cos, cos, jnp.ones_like(zeros)], axis=2)
    sin_full = jnp.concatenate([sin, sin, zeros], axis=2)
    return cos_full, sin_full


def kernel(x, norm_mix_gain, norm_mlp_gain, ret_w_in, ret_w_out, att_w_in, att_q_gain,
           att_k_gain, att_w_out, mlp_w_in, mlp_w_out):
    h = x.reshape(TOKENS, D_MODEL)

    cos, sin, inner_decay, query_decay, key_decay, chunk_decay = _retention_tables()
    proj, ret_wo, mlp0_w1, mlp0_w2 = _ret_inproj(
        h, norm_mix_gain[0:1], ret_w_in.astype(BF16), cos, sin,
        [(ret_w_out, 0), (mlp_w_in, 0), (mlp_w_out, 0)])
    y = _retention(proj, inner_decay, query_decay, key_decay, chunk_decay)
    h = _outproj(y, ret_wo, h)
    h, att_wi, att_wo, mlp1_w1, mlp1_w2 = _mlp(
        h, norm_mlp_gain[0:1], mlp0_w1, mlp0_w2,
        [(att_w_in, 0), (att_w_out, 0), (mlp_w_in, 1), (mlp_w_out, 1)])

    cos_full, sin_full = _attention_tables()
    qkv = _att_inproj(h, norm_mix_gain[1:2], att_wi, att_q_gain[0:1], att_k_gain[0:1],
                      cos_full, sin_full)
    outs, lses = [], []
    for g, (_, dilation) in enumerate(ATT_GROUPS):
        o, l = _attention_group(qkv, g, dilation)
        outs.append(o)
        lses.append(l)
    h = _att_outproj(outs, lses, att_wo, h)
    (h,) = _mlp(h, norm_mlp_gain[1:2], mlp1_w1, mlp1_w2)
    return h.reshape(BATCH, SEQ, D_MODEL)
```

```python
import functools

import jax
import jax.numpy as jnp
from jax import lax
from jax.experimental import pallas as pl
from jax.experimental.pallas import tpu as pltpu

F32 = jnp.float32
BF16 = jnp.bfloat16

D_MODEL = 2048
BATCH = 2
SEQ = 8192
TOKENS = BATCH * SEQ
EPS = 1e-6

RET_HEADS = 8
RET_DK = 256
RET_DV = 512
RET_QK_WIDTH = RET_HEADS * RET_DK
RET_V_WIDTH = RET_HEADS * RET_DV
RET_IN_WIDTH = 2 * RET_QK_WIDTH + 2 * RET_V_WIDTH
RET_CHUNK = 128
RET_ROT_BASE = 10000.0

ATT_GROUPS = ((128, 1), (512, 4), (2048, 16))
N_ATT_GROUPS = 3
ATT_HEAD_DIM = 128
ATT_HEADS = 16
ATT_WIDTH = ATT_HEADS * ATT_HEAD_DIM
ATT_IN_WIDTH = 3 * N_ATT_GROUPS * ATT_WIDTH
ATT_BLOCK = 128
ROPE_DIM = 32
ROPE_HALF = ROPE_DIM // 2
ROPE_THETA = 500000.0

D_FF = 4 * D_MODEL

LANES = 128
VMEM_LIMIT = 56 * 1024 * 1024


def _params(semantics):
    return pltpu.CompilerParams(dimension_semantics=semantics, vmem_limit_bytes=VMEM_LIMIT)


def _rms_norm_bf16(x, gain):
    ms = jnp.mean(x * x, axis=-1, keepdims=True)
    return (x * lax.rsqrt(ms + EPS) * gain).astype(BF16)


SIDE_CAST_BLOCKS = 128


def _side_cast(weights, step_of):
    operands, in_specs, out_specs, out_shapes = [], [], [], []
    for arr, layer in weights:
        _, rows, cols = arr.shape
        block = (None, rows // SIDE_CAST_BLOCKS, cols)
        blk = lambda *ids: jnp.minimum(step_of(*ids), SIDE_CAST_BLOCKS - 1)
        operands.append(arr)
        in_specs.append(pl.BlockSpec(block, lambda *ids, layer=layer, blk=blk: (layer, blk(*ids), 0)))
        out_specs.append(pl.BlockSpec(block, lambda *ids, blk=blk: (0, blk(*ids), 0)))
        out_shapes.append(jax.ShapeDtypeStruct((1, rows, cols), BF16))
    return operands, in_specs, out_specs, out_shapes


def _run_side_casts(src_refs, dst_refs):
    for src, dst in zip(src_refs, dst_refs):
        dst[...] = src[...].astype(BF16)


RET_ROW_PARTS = 2


def _ret_inproj_kernel(h_ref, gain_ref, w_ref, cos_ref, sin_ref, *rest,
                       n_side, n_q_tiles, n_rot_tiles, n_plain_tiles, heads_per_tile):
    side_in, o_ref, side_out, hn_ref = (rest[:n_side], rest[n_side],
                                        rest[n_side + 1:2 * n_side + 1], rest[2 * n_side + 1])
    j = pl.program_id(1)

    @pl.when(j == 0)
    def _():
        hn_ref[...] = _rms_norm_bf16(h_ref[...], gain_ref[...])

    row_part = o_ref.shape[0] // RET_ROW_PARTS
    units = [(hh, r) for hh in range(heads_per_tile) for r in range(RET_ROW_PARTS)]

    def unit_dot(hh, r):
        return jnp.dot(hn_ref[r * row_part:(r + 1) * row_part, :],
                       w_ref[:, hh * RET_DK:(hh + 1) * RET_DK], preferred_element_type=F32)

    @pl.when((j >= n_rot_tiles) & (j < n_plain_tiles))
    def _():
        _run_side_casts(side_in, side_out)
        for hh, r in units:
            o_ref[r * row_part:(r + 1) * row_part, hh * RET_DK:(hh + 1) * RET_DK] = (
                unit_dot(hh, r).astype(o_ref.dtype))

    @pl.when(j >= n_plain_tiles)
    def _():
        _run_side_casts(side_in, side_out)
        for hh, r in units:
            gate = unit_dot(hh, r)
            half_gate = 0.5 * gate
            o_ref[r * row_part:(r + 1) * row_part, hh * RET_DK:(hh + 1) * RET_DK] = (
                half_gate + half_gate * jnp.tanh(half_gate)).astype(o_ref.dtype)

    @pl.when(j < n_rot_tiles)
    def _():
        _run_side_casts(side_in, side_out)
        scale = jnp.where(j >= n_q_tiles, RET_DK ** -0.5, 1.0).astype(F32)
        half = RET_DK // 2
        for hh, r in units:
            rows = slice(r * row_part, (r + 1) * row_part)
            c = cos_ref[rows, :] * scale
            s = sin_ref[rows, :] * scale
            acc = unit_dot(hh, r)
            lo = hh * RET_DK
            x1 = acc[:, :half]
            x2 = acc[:, half:]
            o_ref[rows, lo:lo + half] = (x1 * c - x2 * s).astype(o_ref.dtype)
            o_ref[rows, lo + half:lo + RET_DK] = (x2 * c + x1 * s).astype(o_ref.dtype)


def _ret_inproj(h, gain, w, cos, sin, side_weights, *, tm=1024, tn=1024):
    n_pos_blocks = SEQ // tm
    n_col_tiles = RET_IN_WIDTH // tn
    side_ops, side_in, side_out, side_shapes = _side_cast(
        side_weights, lambda i, j: i * n_col_tiles + j)
    kern = functools.partial(
        _ret_inproj_kernel, n_side=len(side_ops), n_q_tiles=RET_QK_WIDTH // tn,
        n_rot_tiles=2 * RET_QK_WIDTH // tn,
        n_plain_tiles=(2 * RET_QK_WIDTH + RET_V_WIDTH) // tn, heads_per_tile=tn // RET_DK)
    return pl.pallas_call(
        kern,
        grid=(TOKENS // tm, n_col_tiles),
        in_specs=[
            pl.BlockSpec((tm, D_MODEL), lambda i, j: (i, 0)),
            pl.BlockSpec((1, D_MODEL), lambda i, j: (0, 0)),
            pl.BlockSpec((None, D_MODEL, tn), lambda i, j: (0, 0, j)),
            pl.BlockSpec((tm, RET_DK // 2), lambda i, j: (i % n_pos_blocks, 0)),
            pl.BlockSpec((tm, RET_DK // 2), lambda i, j: (i % n_pos_blocks, 0)),
        ] + side_in,
        out_specs=[pl.BlockSpec((tm, tn), lambda i, j: (i, j))] + side_out,
        out_shape=[jax.ShapeDtypeStruct((TOKENS, RET_IN_WIDTH), BF16)] + side_shapes,
        scratch_shapes=[pltpu.VMEM((tm, D_MODEL), BF16)],
        compiler_params=_params(("arbitrary", "arbitrary")),
        name="ret_inproj",
    )(h, gain, w, cos, sin, *side_ops)


RET_BLOCK = 256


def _retention_kernel(q_ref, k_ref, v_ref, g_ref, dec_ref, qd_ref, kd_ref, cd_ref, o_ref,
                      state_ref, upd_ref, *, n_chunks):
    @pl.when(pl.program_id(2) == 0)
    def _():
        state_ref[...] = jnp.zeros_like(state_ref)

    dec = dec_ref[...]
    qd = qd_ref[...]
    kd = kd_ref[...]
    cd = cd_ref[...]
    chunk = lambda c: slice(c * RET_BLOCK, (c + 1) * RET_BLOCK)

    for c in range(n_chunks):
        k_dec = (k_ref[chunk(c), :].astype(F32) * kd).astype(BF16)
        upd_ref[c] = lax.dot_general(k_dec, v_ref[chunk(c), :], (((0,), (0,)), ((), ())),
                                     preferred_element_type=F32)

    for c in range(n_chunks):
        qc, kc, vc = q_ref[chunk(c), :], k_ref[chunk(c), :], v_ref[chunk(c), :]
        state = state_ref[...]
        cross = jnp.dot(qc, state.astype(BF16), preferred_element_type=F32) * qd
        state_ref[...] = state * cd + upd_ref[c]
        scores = lax.dot_general(qc, kc, (((1,), (1,)), ((), ())),
                                 preferred_element_type=F32) * dec
        inner = jnp.dot(scores.astype(BF16), vc, preferred_element_type=F32)
        out = inner + cross
        ms = jnp.mean(out * out, axis=-1, keepdims=True)
        y = out * lax.rsqrt(ms + EPS) * g_ref[chunk(c), :].astype(F32)
        o_ref[chunk(c), :] = y.astype(o_ref.dtype)


def _retention(proj, dec, qd, kd, cd, *, rows=1024):
    n_row_blocks = SEQ // rows
    n_chunks = rows // RET_BLOCK
    k_off = RET_QK_WIDTH // RET_DK
    v_off = 2 * RET_QK_WIDTH // RET_DV
    g_off = v_off + RET_V_WIDTH // RET_DV
    row = lambda b, h, r: b * n_row_blocks + r
    kern = functools.partial(_retention_kernel, n_chunks=n_chunks)
    return pl.pallas_call(
        kern,
        grid=(BATCH, RET_HEADS, n_row_blocks),
        in_specs=[
            pl.BlockSpec((rows, RET_DK), lambda b, h, r: (row(b, h, r), h)),
            pl.BlockSpec((rows, RET_DK), lambda b, h, r: (row(b, h, r), k_off + h)),
            pl.BlockSpec((rows, RET_DV), lambda b, h, r: (row(b, h, r), v_off + h)),
            pl.BlockSpec((rows, RET_DV), lambda b, h, r: (row(b, h, r), g_off + h)),
            pl.BlockSpec((None, RET_BLOCK, RET_BLOCK), lambda b, h, r: (h, 0, 0)),
            pl.BlockSpec((None, RET_BLOCK, 1), lambda b, h, r: (h, 0, 0)),
            pl.BlockSpec((None, RET_BLOCK, 1), lambda b, h, r: (h, 0, 0)),
            pl.BlockSpec((None, 1, RET_DV), lambda b, h, r: (h, 0, 0)),
        ],
        out_specs=pl.BlockSpec((rows, RET_DV), lambda b, h, r: (row(b, h, r), h)),
        out_shape=jax.ShapeDtypeStruct((TOKENS, RET_V_WIDTH), BF16),
        scratch_shapes=[pltpu.VMEM((RET_DK, RET_DV), F32),
                        pltpu.VMEM((n_chunks, RET_DK, RET_DV), F32)],
        compiler_params=_params(("parallel", "parallel", "arbitrary")),
        name="retention",
    )(proj, proj, proj, proj, dec, qd, kd, cd)


def _outproj_kernel(y_ref, w_ref, res_ref, o_ref):
    o_ref[...] = res_ref[...] + jnp.dot(y_ref[...], w_ref[...], preferred_element_type=F32)


def _outproj(y, w, res, *, tm=512, tn=1024):
    k = y.shape[1]
    return pl.pallas_call(
        _outproj_kernel,
        grid=(D_MODEL // tn, TOKENS // tm),
        in_specs=[
            pl.BlockSpec((tm, k), lambda j, i: (i, 0)),
            pl.BlockSpec((None, k, tn), lambda j, i: (0, 0, j)),
            pl.BlockSpec((tm, tn), lambda j, i: (i, j)),
        ],
        out_specs=pl.BlockSpec((tm, tn), lambda j, i: (i, j)),
        out_shape=jax.ShapeDtypeStruct((TOKENS, D_MODEL), F32),
        compiler_params=_params(("parallel", "parallel")),
        name="outproj",
    )(y, w, res)


def _mlp_kernel(h_ref, gain_ref, w1_ref, w2_ref, *rest, n_side):
    side_in, o_ref, side_out, hn_ref = (rest[:n_side], rest[n_side],
                                        rest[n_side + 1:2 * n_side + 1], rest[2 * n_side + 1])
    @pl.when(pl.program_id(1) == 0)
    def _():
        h = h_ref[...]
        hn_ref[...] = _rms_norm_bf16(h, gain_ref[...])
        o_ref[...] = h

    _run_side_casts(side_in, side_out)
    a = jnp.dot(hn_ref[...], w1_ref[...], preferred_element_type=F32)
    act = jnp.square(jnp.maximum(a, 0.0)).astype(BF16)
    o_ref[...] += jnp.dot(act, w2_ref[...], preferred_element_type=F32)


def _mlp(h, gain, w1, w2, side_weights=(), *, tm=512, tf=1024):
    n_ff_tiles = D_FF // tf
    side_ops, side_in, side_out, side_shapes = _side_cast(
        side_weights, lambda i, f: i * n_ff_tiles + f)
    return pl.pallas_call(
        functools.partial(_mlp_kernel, n_side=len(side_ops)),
        grid=(TOKENS // tm, n_ff_tiles),
        in_specs=[
            pl.BlockSpec((tm, D_MODEL), lambda i, f: (i, 0)),
            pl.BlockSpec((1, D_MODEL), lambda i, f: (0, 0)),
            pl.BlockSpec((None, D_MODEL, tf), lambda i, f: (0, 0, f)),
            pl.BlockSpec((None, tf, D_MODEL), lambda i, f: (0, f, 0)),
        ] + side_in,
        out_specs=[pl.BlockSpec((tm, D_MODEL), lambda i, f: (i, 0))] + side_out,
        out_shape=[jax.ShapeDtypeStruct((TOKENS, D_MODEL), F32)] + side_shapes,
        scratch_shapes=[pltpu.VMEM((tm, D_MODEL), BF16)],
        compiler_params=_params(("arbitrary", "arbitrary")),
        name="mlp",
    )(h, gain, w1, w2, *side_ops)


ATT_TILE = 256
ATT_SUB = 256
ATT_ROW_PARTS = 2


def _residue_major_matrix(dilation):
    r = jnp.arange(ATT_TILE)
    n = ATT_TILE // dilation
    src = (r % n) * dilation + r // n
    return (src[:, None] == r[None, :]).astype(BF16)


def _head_aux_matrix():
    i = jnp.arange(ATT_SUB)
    row, col = i[:, None], i[None, :]
    same_head = (row // ATT_HEAD_DIM) == (col // ATT_HEAD_DIM)
    r, c = row % ATT_HEAD_DIM, col % ATT_HEAD_DIM
    minus = same_head & (c < ROPE_HALF) & (r == c + ROPE_HALF)
    plus = same_head & (r < ROPE_HALF) & (c == r + ROPE_HALF)
    return (plus.astype(F32) - minus.astype(F32)).astype(BF16)


def _att_inproj_kernel(h_ref, gain_ref, w_ref, qg_ref, kg_ref, cos_ref, sin_ref, aux_ref,
                       p1_ref, p2_ref, o_ref, hn_ref, *, tiles_per_unit, n_q_tiles, n_qk_tiles):
    j = pl.program_id(1)
    tm, tn = o_ref.shape

    @pl.when(j == 0)
    def _():
        hn_ref[0] = _rms_norm_bf16(h_ref[...], gain_ref[...])
        for g, p_ref in ((1, p1_ref), (2, p2_ref)):
            for t in range(tm // ATT_TILE):
                rows = slice(t * ATT_TILE, (t + 1) * ATT_TILE)
                hn_ref[g, rows, :] = jnp.dot(p_ref[...], hn_ref[0, rows, :],
                                             preferred_element_type=F32).astype(BF16)

    g = (j // tiles_per_unit) % N_ATT_GROUPS

    row_part = tm // ATT_ROW_PARTS
    units = [(k, r) for r in range(ATT_ROW_PARTS) for k in range(tn // ATT_SUB)]

    def unit_dot(k, r):
        return jnp.dot(hn_ref[g, r * row_part:(r + 1) * row_part, :],
                       w_ref[:, k * ATT_SUB:(k + 1) * ATT_SUB], preferred_element_type=F32)

    @pl.when(j >= n_qk_tiles)
    def _():
        for k, r in units:
            o_ref[r * row_part:(r + 1) * row_part, k * ATT_SUB:(k + 1) * ATT_SUB] = (
                unit_dot(k, r).astype(o_ref.dtype))

    @pl.when(j < n_qk_tiles)
    def _():
        is_q = j < n_q_tiles
        head_gain = jnp.where(is_q, qg_ref[...] * (ATT_HEAD_DIM ** -0.5 * LOG2_E), kg_ref[...])

        def epilogue(k, r, acc):
            rows = slice(r * row_part, (r + 1) * row_part)
            c = cos_ref[rows, :]
            s = sin_ref[rows, :]
            xg = acc * jnp.concatenate([head_gain, head_gain], axis=1)
            partner = jnp.dot(xg.astype(BF16), aux_ref[...], preferred_element_type=F32)
            for hh in range(ATT_SUB // ATT_HEAD_DIM):
                cols = slice(hh * ATT_HEAD_DIM, (hh + 1) * ATT_HEAD_DIM)
                x = acc[:, cols]
                inv = lax.rsqrt(jnp.mean(x * x, axis=-1, keepdims=True) + EPS)
                rot = (xg[:, cols] * c + partner[:, cols] * s) * inv
                lo = k * ATT_SUB + hh * ATT_HEAD_DIM
                o_ref[rows, lo:lo + ATT_HEAD_DIM] = rot.astype(o_ref.dtype)

        acc = unit_dot(*units[0])
        for n, (k, r) in enumerate(units):
            nxt = unit_dot(*units[n + 1]) if n + 1 < len(units) else None
            epilogue(k, r, acc)
            acc = nxt


def _att_inproj(h, gain, w, q_gain, k_gain, cos, sin, *, tm=1024, tn=1024):
    n_pos_blocks = SEQ // tm
    group_width = N_ATT_GROUPS * ATT_WIDTH
    tiles_per_unit = ATT_WIDTH // tn
    kern = functools.partial(
        _att_inproj_kernel, tiles_per_unit=tiles_per_unit, n_q_tiles=group_width // tn,
        n_qk_tiles=2 * group_width // tn)
    pos_spec = pl.BlockSpec(
        (None, tm, ATT_HEAD_DIM),
        lambda i, j: ((j // tiles_per_unit) % N_ATT_GROUPS, i % n_pos_blocks, 0))
    return pl.pallas_call(
        kern,
        grid=(TOKENS // tm, ATT_IN_WIDTH // tn),
        in_specs=[
            pl.BlockSpec((tm, D_MODEL), lambda i, j: (i, 0)),
            pl.BlockSpec((1, D_MODEL), lambda i, j: (0, 0)),
            pl.BlockSpec((None, D_MODEL, tn), lambda i, j: (0, 0, j)),
            pl.BlockSpec((1, ATT_HEAD_DIM), lambda i, j: (0, 0)),
            pl.BlockSpec((1, ATT_HEAD_DIM), lambda i, j: (0, 0)),
            pos_spec, pos_spec,
            pl.BlockSpec((2 * ATT_HEAD_DIM, 2 * ATT_HEAD_DIM), lambda i, j: (0, 0)),
            pl.BlockSpec((ATT_TILE, ATT_TILE), lambda i, j: (0, 0)),
            pl.BlockSpec((ATT_TILE, ATT_TILE), lambda i, j: (0, 0)),
        ],
        out_specs=pl.BlockSpec((tm, tn), lambda i, j: (i, j)),
        out_shape=jax.ShapeDtypeStruct((TOKENS, ATT_IN_WIDTH), BF16),
        scratch_shapes=[pltpu.VMEM((N_ATT_GROUPS, tm, D_MODEL), BF16)],
        compiler_params=_params(("parallel", "arbitrary")),
        name="att_inproj",
    )(h, gain, w, q_gain, k_gain, cos, sin, _head_aux_matrix(),
      _residue_major_matrix(ATT_GROUPS[1][1]), _residue_major_matrix(ATT_GROUPS[2][1]))


ATT_STEP = 4
LOG2_E = 1.4426950408889634
LN_2 = 0.6931471805599453


def _attention_kernel(q_ref, k_ref, v_ref, o_ref, lse_ref, kp_ref, vp_ref, s_ref):
    n = pl.program_id(2)

    @pl.when(n == 0)
    def _():
        kp_ref[...] = jnp.zeros_like(kp_ref)
        vp_ref[...] = jnp.zeros_like(vp_ref)

    def block_rows(ref, b):
        if len(ref.shape) == 2:
            return (slice(b * ATT_BLOCK, (b + 1) * ATT_BLOCK),), (ATT_BLOCK,)
        n_sub = ref.shape[1]
        if n_sub >= ATT_BLOCK:
            tile, off = divmod(b * ATT_BLOCK, n_sub)
            return (tile, slice(off, off + ATT_BLOCK)), (ATT_BLOCK,)
        per = ATT_BLOCK // n_sub
        return (slice(b * per, (b + 1) * per), slice(None)), (per, n_sub)

    def load(ref, b, cols):
        idx, _ = block_rows(ref, b)
        x = ref[idx + (cols,)]
        return x.reshape(ATT_BLOCK, x.shape[-1])

    def store(ref, b, cols, value):
        idx, lead = block_rows(ref, b)
        ref[idx + (cols,)] = value.reshape(lead + (value.shape[-1],))

    qi = lax.broadcasted_iota(jnp.int32, (ATT_BLOCK, 2 * ATT_BLOCK), 0)
    kj = lax.broadcasted_iota(jnp.int32, (ATT_BLOCK, 2 * ATT_BLOCK), 1)
    band = (kj >= qi) & (kj <= qi + ATT_BLOCK)
    first_band = band & ((n > 0) | (kj >= ATT_BLOCK))
    lane = lax.broadcasted_iota(jnp.int32, (ATT_BLOCK, LANES), 1)
    contract_last = (((1,), (1,)), ((), ()))
    head_cols = [slice(h * ATT_HEAD_DIM, (h + 1) * ATT_HEAD_DIM) for h in range(ATT_HEADS)]

    for b in range(ATT_STEP):
        for h, cols in enumerate(head_cols):
            q = load(q_ref, b, cols)
            k_prev = kp_ref[:, cols] if b == 0 else load(k_ref, b - 1, cols)
            s_ref[b * ATT_HEADS + h, :, :ATT_BLOCK] = lax.dot_general(
                q, k_prev, contract_last, preferred_element_type=F32)
            s_ref[b * ATT_HEADS + h, :, ATT_BLOCK:] = lax.dot_general(
                q, load(k_ref, b, cols), contract_last, preferred_element_type=F32)

    for b in range(ATT_STEP):
        valid = first_band if b == 0 else band
        max_tile = jnp.zeros((ATT_BLOCK, LANES), F32)
        sum_tile = jnp.ones((ATT_BLOCK, LANES), F32)
        for h, cols in enumerate(head_cols):
            s = jnp.where(valid, s_ref[b * ATT_HEADS + h], -jnp.inf)
            m = jnp.max(s, axis=-1, keepdims=True)
            p = jnp.exp2(s - m)
            denom = jnp.sum(p, axis=-1, keepdims=True)
            pb = p.astype(BF16)
            v_prev = vp_ref[:, cols] if b == 0 else load(v_ref, b - 1, cols)
            pv = (jnp.dot(pb[:, :ATT_BLOCK], v_prev, preferred_element_type=F32)
                  + jnp.dot(pb[:, ATT_BLOCK:], load(v_ref, b, cols), preferred_element_type=F32))
            store(o_ref, b, cols, (pv / denom).astype(o_ref.dtype))
            max_tile = jnp.where(lane == h, m, max_tile)
            sum_tile = jnp.where(lane == h, denom, sum_tile)
        store(lse_ref, b, slice(None), (max_tile + jnp.log2(sum_tile)) * LN_2)

    last = ATT_STEP - 1
    kp_ref[...] = load(k_ref, last, slice(None))
    vp_ref[...] = load(v_ref, last, slice(None))


def _attention_group(qkv, group, dilation):
    n_sub = ATT_TILE // dilation
    step = ATT_STEP * ATT_BLOCK
    n_steps = SEQ // dilation // step
    n_tiles = TOKENS // ATT_TILE
    tiles_per_batch = SEQ // ATT_TILE
    q_unit, k_unit, v_unit = group, N_ATT_GROUPS + group, 2 * N_ATT_GROUPS + group
    view = qkv.reshape(n_tiles, dilation, n_sub, ATT_IN_WIDTH)

    if n_sub >= step:
        per_tile = n_sub // step
        lead = (None, None, step)
        index = lambda b, c, n: (b * tiles_per_batch + n // per_tile, c, n % per_tile)
    else:
        span = step // n_sub
        lead = (span, None, n_sub)
        index = lambda b, c, n: (b * (tiles_per_batch // span) + n, c, 0)

    def spec(width, unit):
        return pl.BlockSpec(lead + (width,), lambda b, c, n: index(b, c, n) + (unit,))

    return pl.pallas_call(
        _attention_kernel,
        grid=(BATCH, dilation, n_steps),
        in_specs=[spec(ATT_WIDTH, q_unit), spec(ATT_WIDTH, k_unit), spec(ATT_WIDTH, v_unit)],
        out_specs=[spec(ATT_WIDTH, 0), spec(LANES, 0)],
        out_shape=[
            jax.ShapeDtypeStruct((n_tiles, dilation, n_sub, ATT_WIDTH), BF16),
            jax.ShapeDtypeStruct((n_tiles, dilation, n_sub, LANES), F32),
        ],
        scratch_shapes=[
            pltpu.VMEM((ATT_BLOCK, ATT_WIDTH), BF16),
            pltpu.VMEM((ATT_BLOCK, ATT_WIDTH), BF16),
            pltpu.VMEM((ATT_STEP * ATT_HEADS, ATT_BLOCK, 2 * ATT_BLOCK), F32),
        ],
        compiler_params=_params(("parallel", "parallel", "arbitrary")),
        name=f"attention_d{dilation}",
    )(view, view, view)


OUT_TM = 256


def _att_outproj_kernel(o0_ref, o1_ref, o2_ref, l0_ref, l1_ref, l2_ref, p1_ref, p2_ref, w_ref,
                        res_ref, out_ref, lse_ref, even_ref, odd_ref):
    s = pl.program_id(0)

    @pl.when(s == 0)
    def _():
        odd_ref[...] = jnp.zeros_like(odd_ref)

    def step(merged_prev_ref, merged_ref):
        out_ref[...] = res_ref[...] + jnp.dot(merged_prev_ref[...], w_ref[...],
                                              preferred_element_type=F32)
        for k, l_ref in enumerate((l1_ref, l2_ref)):
            d, n = l_ref.shape[0], l_ref.shape[1]
            for c in range(d):
                lse_ref[k, pl.ds(c, n, stride=d), :] = l_ref[c]
        l0, l1, l2 = l0_ref[0], lse_ref[0], lse_ref[1]
        top = jnp.maximum(jnp.maximum(l0, l1), l2)
        e0, e1, e2 = jnp.exp(l0 - top), jnp.exp(l1 - top), jnp.exp(l2 - top)
        total = e0 + e1 + e2
        w0, w1, w2 = e0 / total, e1 / total, e2 / total
        o1 = jnp.dot(p1_ref[...], o1_ref[...].reshape(OUT_TM, ATT_WIDTH),
                     preferred_element_type=F32)
        o2 = jnp.dot(p2_ref[...], o2_ref[...].reshape(OUT_TM, ATT_WIDTH),
                     preferred_element_type=F32)
        for h in range(ATT_HEADS):
            cols = slice(h * ATT_HEAD_DIM, (h + 1) * ATT_HEAD_DIM)
            merged = (w0[:, h:h + 1] * o0_ref[0, :, cols].astype(F32)
                      + w1[:, h:h + 1] * o1[:, cols]
                      + w2[:, h:h + 1] * o2[:, cols])
            merged_ref[:, cols] = merged.astype(BF16)

    @pl.when(s % 2 == 0)
    def _():
        step(odd_ref, even_ref)

    @pl.when(s % 2 == 1)
    def _():
        step(even_ref, odd_ref)


def _token_order_matrix(dilation):
    t = jnp.arange(OUT_TM)
    src = (t % dilation) * (OUT_TM // dilation) + t // dilation
    return (src[:, None] == t[None, :]).astype(BF16)


def _att_outproj(outs, lses, w, res):
    tm = OUT_TM
    per_tile = ATT_TILE // tm

    n_row_tiles = TOKENS // tm
    merge_tile = lambda s: jnp.minimum(s, n_row_tiles - 1)
    project_tile = lambda s: jnp.maximum(s - 1, 0)

    def spec(dilation, width):
        return pl.BlockSpec(
            (None, dilation, tm // dilation, width),
            lambda s: (merge_tile(s) // per_tile, 0, merge_tile(s) % per_tile, 0))

    dils = [d for _, d in ATT_GROUPS]
    const = lambda shape: pl.BlockSpec(shape, lambda s: (0, 0))
    return pl.pallas_call(
        _att_outproj_kernel,
        grid=(n_row_tiles + 1,),
        in_specs=[spec(d, ATT_WIDTH) for d in dils] + [spec(d, LANES) for d in dils] + [
            const((tm, tm)), const((tm, tm)),
            pl.BlockSpec((None, ATT_WIDTH, D_MODEL), lambda s: (0, 0, 0)),
            pl.BlockSpec((tm, D_MODEL), lambda s: (project_tile(s), 0))],
        out_specs=pl.BlockSpec((tm, D_MODEL), lambda s: (project_tile(s), 0)),
        out_shape=jax.ShapeDtypeStruct((TOKENS, D_MODEL), F32),
        scratch_shapes=[pltpu.VMEM((N_ATT_GROUPS - 1, tm, LANES), F32),
                        pltpu.VMEM((tm, ATT_WIDTH), BF16),
                        pltpu.VMEM((tm, ATT_WIDTH), BF16)],
        compiler_params=_params(("arbitrary",)),
        name="att_outproj",
    )(*outs, *lses, _token_order_matrix(dils[1]), _token_order_matrix(dils[2]), w, res)


def _cos_sin(pos_lo, inv_freq):
    pos_hi = jnp.arange(SEQ // ATT_TILE, dtype=F32) * ATT_TILE
    ang_hi = pos_hi[:, None] * inv_freq
    ang_lo = pos_lo[..., None] * inv_freq
    ch, sh, cl, sl = lax.optimization_barrier(
        (jnp.cos(ang_hi), jnp.sin(ang_hi), jnp.cos(ang_lo), jnp.sin(ang_lo)))
    ch, sh = ch[:, None, :], sh[:, None, :]
    cl, sl = cl[..., None, :, :], sl[..., None, :, :]
    shape = pos_lo.shape[:-1] + (SEQ, inv_freq.shape[0])
    return (ch * cl - sh * sl).reshape(shape), (sh * cl + ch * sl).reshape(shape)


def _retention_tables():
    inv_freq = 1.0 / (RET_ROT_BASE ** jnp.linspace(0.0, 1.0, RET_DK // 2, dtype=F32))
    cos, sin = _cos_sin(jnp.arange(ATT_TILE, dtype=F32), inv_freq)
    log_gamma = jnp.log(1.0 - 2.0 ** (-5.0 - jnp.arange(RET_HEADS, dtype=F32)))
    idx = jnp.arange(RET_BLOCK, dtype=F32)
    pow_i, pow_neg_j = lax.optimization_barrier(
        (jnp.exp(log_gamma[:, None] * idx[None, :]), jnp.exp(-log_gamma[:, None] * idx[None, :])))
    inner_decay = jnp.where(idx[:, None] >= idx[None, :],
                            pow_i[:, :, None] * pow_neg_j[:, None, :], 0.0)
    query_decay = jnp.exp(log_gamma[:, None] * (idx[None, :] + 1.0))[:, :, None]
    key_decay = jnp.exp(log_gamma[:, None] * (RET_BLOCK - 1.0 - idx[None, :]))[:, :, None]
    chunk_decay = jnp.broadcast_to(jnp.exp(log_gamma * RET_BLOCK)[:, None, None],
                                   (RET_HEADS, 1, RET_DV))
    return cos, sin, inner_decay, query_decay, key_decay, chunk_decay


def _attention_tables():
    pos_lo = jnp.arange(ATT_TILE, dtype=F32)
    pos_lo = jnp.stack([pos_lo.reshape(ATT_TILE // d, d).T.reshape(ATT_TILE)
                        for _, d in ATT_GROUPS])
    inv_freq = ROPE_THETA ** (-jnp.arange(0, ROPE_DIM, 2, dtype=F32) / ROPE_DIM)
    cos, sin = _cos_sin(pos_lo, inv_freq)
    zeros = jnp.zeros((N_ATT_GROUPS, SEQ, ATT_HEAD_DIM - ROPE_DIM), F32)
    cos_full = jnp.concatenate([cos, cos, jnp.ones_like(zeros)], axis=2)
    sin_full = jnp.concatenate([sin, sin, zeros], axis=2)
    return cos_full, sin_full


def kernel(x, norm_mix_gain, norm_mlp_gain, ret_w_in, ret_w_out, att_w_in, att_q_gain,
           att_k_gain, att_w_out, mlp_w_in, mlp_w_out):
    h = x.reshape(TOKENS, D_MODEL)

    cos, sin, inner_decay, query_decay, key_decay, chunk_decay = _retention_tables()
    proj, ret_wo, mlp0_w1, mlp0_w2 = _ret_inproj(
        h, norm_mix_gain[0:1], ret_w_in.astype(BF16), cos, sin,
        [(ret_w_out, 0), (mlp_w_in, 0), (mlp_w_out, 0)])
    y = _retention(proj, inner_decay, query_decay, key_decay, chunk_decay)
    h = _outproj(y, ret_wo, h)
    h, att_wi, att_wo, mlp1_w1, mlp1_w2 = _mlp(
        h, norm_mlp_gain[0:1], mlp0_w1, mlp0_w2,
        [(att_w_in, 0), (att_w_out, 0), (mlp_w_in, 1), (mlp_w_out, 1)])

    cos_full, sin_full = _attention_tables()
    qkv = _att_inproj(h, norm_mix_gain[1:2], att_wi, att_q_gain[0:1], att_k_gain[0:1],
                      cos_full, sin_full)
    outs, lses = [], []
    for g, (_, dilation) in enumerate(ATT_GROUPS):
        o, l = _attention_group(qkv, g, dilation)
        outs.append(o)
        lses.append(l)
    h = _att_outproj(outs, lses, att_wo, h)
    (h,) = _mlp(h, norm_mlp_gain[1:2], mlp1_w1, mlp1_w2)
    return h.reshape(BATCH, SEQ, D_MODEL)
```

```python
import functools

import jax
import jax.numpy as jnp
from jax import lax
from jax.experimental import pallas as pl
from jax.experimental.pallas import tpu as pltpu

F32 = jnp.float32
BF16 = jnp.bfloat16

D_MODEL = 2048
BATCH = 2
SEQ = 8192
TOKENS = BATCH * SEQ
EPS = 1e-6

RET_HEADS = 8
RET_DK = 256
RET_DV = 512
RET_QK_WIDTH = RET_HEADS * RET_DK
RET_V_WIDTH = RET_HEADS * RET_DV
RET_IN_WIDTH = 2 * RET_QK_WIDTH + 2 * RET_V_WIDTH
RET_CHUNK = 128
RET_ROT_BASE = 10000.0

ATT_GROUPS = ((128, 1), (512, 4), (2048, 16))
N_ATT_GROUPS = 3
ATT_HEAD_DIM = 128
ATT_HEADS = 16
ATT_WIDTH = ATT_HEADS * ATT_HEAD_DIM
ATT_IN_WIDTH = 3 * N_ATT_GROUPS * ATT_WIDTH
ATT_BLOCK = 128
ROPE_DIM = 32
ROPE_HALF = ROPE_DIM // 2
ROPE_THETA = 500000.0

D_FF = 4 * D_MODEL

LANES = 128
VMEM_LIMIT = 56 * 1024 * 1024


def _params(semantics):
    return pltpu.CompilerParams(dimension_semantics=semantics, vmem_limit_bytes=VMEM_LIMIT)


def _rms_norm_bf16(x, gain):
    ms = jnp.mean(x * x, axis=-1, keepdims=True)
    return (x * lax.rsqrt(ms + EPS) * gain).astype(BF16)


SIDE_CAST_BLOCKS = 128


def _side_cast(weights, step_of):
    operands, in_specs, out_specs, out_shapes = [], [], [], []
    for arr, layer in weights:
        _, rows, cols = arr.shape
        block = (None, rows // SIDE_CAST_BLOCKS, cols)
        blk = lambda *ids: jnp.minimum(step_of(*ids), SIDE_CAST_BLOCKS - 1)
        operands.append(arr)
        in_specs.append(pl.BlockSpec(block, lambda *ids, layer=layer, blk=blk: (layer, blk(*ids), 0)))
        out_specs.append(pl.BlockSpec(block, lambda *ids, blk=blk: (0, blk(*ids), 0)))
        out_shapes.append(jax.ShapeDtypeStruct((1, rows, cols), BF16))
    return operands, in_specs, out_specs, out_shapes


def _run_side_casts(src_refs, dst_refs):
    for src, dst in zip(src_refs, dst_refs):
        dst[...] = src[...].astype(BF16)


RET_ROW_PARTS = 2


def _ret_inproj_kernel(h_ref, gain_ref, w_ref, cos_ref, sin_ref, *rest,
                       n_side, n_q_tiles, n_rot_tiles, n_plain_tiles, heads_per_tile):
    side_in, o_ref, side_out, hn_ref = (rest[:n_side], rest[n_side],
                                        rest[n_side + 1:2 * n_side + 1], rest[2 * n_side + 1])
    j = pl.program_id(1)

    @pl.when(j == 0)
    def _():
        hn_ref[...] = _rms_norm_bf16(h_ref[...], gain_ref[...])

    row_part = o_ref.shape[0] // RET_ROW_PARTS
    units = [(hh, r) for hh in range(heads_per_tile) for r in range(RET_ROW_PARTS)]

    def unit_dot(hh, r):
        return jnp.dot(hn_ref[r * row_part:(r + 1) * row_part, :],
                       w_ref[:, hh * RET_DK:(hh + 1) * RET_DK], preferred_element_type=F32)

    @pl.when((j >= n_rot_tiles) & (j < n_plain_tiles))
    def _():
        _run_side_casts(side_in, side_out)
        for hh, r in units:
            o_ref[r * row_part:(r + 1) * row_part, hh * RET_DK:(hh + 1) * RET_DK] = (
                unit_dot(hh, r).astype(o_ref.dtype))

    @pl.when(j >= n_plain_tiles)
    def _():
        _run_side_casts(side_in, side_out)
        for hh, r in units:
            gate = unit_dot(hh, r)
            half_gate = 0.5 * gate
            o_ref[r * row_part:(r + 1) * row_part, hh * RET_DK:(hh + 1) * RET_DK] = (
                half_gate + half_gate * jnp.tanh(half_gate)).astype(o_ref.dtype)

    @pl.when(j < n_rot_tiles)
    def _():
        _run_side_casts(side_in, side_out)
        scale = jnp.where(j >= n_q_tiles, RET_DK ** -0.5, 1.0).astype(F32)
        half = RET_DK // 2
        for hh, r in units:
            rows = slice(r * row_part, (r + 1) * row_part)
            c = cos_ref[rows, :] * scale
            s = sin_ref[rows, :] * scale
            acc = unit_dot(hh, r)
            lo = hh * RET_DK
            x1 = acc[:, :half]
            x2 = acc[:, half:]
            o_ref[rows, lo:lo + half] = (x1 * c - x2 * s).astype(o_ref.dtype)
            o_ref[rows, lo + half:lo + RET_DK] = (x2 * c + x1 * s).astype(o_ref.dtype)


def _ret_inproj(h, gain, w, cos, sin, side_weights, *, tm=1024, tn=1024):
    n_pos_blocks = SEQ // tm
    n_col_tiles = RET_IN_WIDTH // tn
    side_ops, side_in, side_out, side_shapes = _side_cast(
        side_weights, lambda i, j: i * n_col_tiles + j)
    kern = functools.partial(
        _ret_inproj_kernel, n_side=len(side_ops), n_q_tiles=RET_QK_WIDTH // tn,
        n_rot_tiles=2 * RET_QK_WIDTH // tn,
        n_plain_tiles=(2 * RET_QK_WIDTH + RET_V_WIDTH) // tn, heads_per_tile=tn // RET_DK)
    return pl.pallas_call(
        kern,
        grid=(TOKENS // tm, n_col_tiles),
        in_specs=[
            pl.BlockSpec((tm, D_MODEL), lambda i, j: (i, 0)),
            pl.BlockSpec((1, D_MODEL), lambda i, j: (0, 0)),
            pl.BlockSpec((None, D_MODEL, tn), lambda i, j: (0, 0, j)),
            pl.BlockSpec((tm, RET_DK // 2), lambda i, j: (i % n_pos_blocks, 0)),
            pl.BlockSpec((tm, RET_DK // 2), lambda i, j: (i % n_pos_blocks, 0)),
        ] + side_in,
        out_specs=[pl.BlockSpec((tm, tn), lambda i, j: (i, j))] + side_out,
        out_shape=[jax.ShapeDtypeStruct((TOKENS, RET_IN_WIDTH), BF16)] + side_shapes,
        scratch_shapes=[pltpu.VMEM((tm, D_MODEL), BF16)],
        compiler_params=_params(("arbitrary", "arbitrary")),
        name="ret_inproj",
    )(h, gain, w, cos, sin, *side_ops)


RET_BLOCK = 256


def _retention_kernel(q_ref, k_ref, v_ref, g_ref, dec_ref, qd_ref, kd_ref, cd_ref, o_ref,
                      state_ref, acc_ref, upd_ref, *, n_chunks):
    @pl.when(pl.program_id(2) == 0)
    def _():
        state_ref[...] = jnp.zeros_like(state_ref)

    dec = dec_ref[...]
    qd = qd_ref[...]
    kd = kd_ref[...]
    cd = cd_ref[...]
    chunk = lambda c: slice(c * RET_BLOCK, (c + 1) * RET_BLOCK)

    for c in range(n_chunks):
        qc, kc, vc = q_ref[chunk(c), :], k_ref[chunk(c), :], v_ref[chunk(c), :]
        scores = lax.dot_general(qc, kc, (((1,), (1,)), ((), ())),
                                 preferred_element_type=F32) * dec
        acc_ref[chunk(c), :] = jnp.dot(scores.astype(BF16), vc, preferred_element_type=F32)
        k_dec = (kc.astype(F32) * kd).astype(BF16)
        upd_ref[c] = lax.dot_general(k_dec, vc, (((0,), (0,)), ((), ())),
                                     preferred_element_type=F32)

    for c in range(n_chunks):
        state = state_ref[...]
        cross = jnp.dot(q_ref[chunk(c), :], state.astype(BF16), preferred_element_type=F32) * qd
        state_ref[...] = state * cd + upd_ref[c]
        out = acc_ref[chunk(c), :] + cross
        ms = jnp.mean(out * out, axis=-1, keepdims=True)
        y = out * lax.rsqrt(ms + EPS) * g_ref[chunk(c), :].astype(F32)
        o_ref[chunk(c), :] = y.astype(o_ref.dtype)


def _retention(proj, dec, qd, kd, cd, *, rows=1024):
    n_row_blocks = SEQ // rows
    n_chunks = rows // RET_BLOCK
    k_off = RET_QK_WIDTH // RET_DK
    v_off = 2 * RET_QK_WIDTH // RET_DV
    g_off = v_off + RET_V_WIDTH // RET_DV
    row = lambda b, h, r: b * n_row_blocks + r
    kern = functools.partial(_retention_kernel, n_chunks=n_chunks)
    return pl.pallas_call(
        kern,
        grid=(BATCH, RET_HEADS, n_row_blocks),
        in_specs=[
            pl.BlockSpec((rows, RET_DK), lambda b, h, r: (row(b, h, r), h)),
            pl.BlockSpec((rows, RET_DK), lambda b, h, r: (row(b, h, r), k_off + h)),
            pl.BlockSpec((rows, RET_DV), lambda b, h, r: (row(b, h, r), v_off + h)),
            pl.BlockSpec((rows, RET_DV), lambda b, h, r: (row(b, h, r), g_off + h)),
            pl.BlockSpec((None, RET_BLOCK, RET_BLOCK), lambda b, h, r: (h, 0, 0)),
            pl.BlockSpec((None, RET_BLOCK, 1), lambda b, h, r: (h, 0, 0)),
            pl.BlockSpec((None, RET_BLOCK, 1), lambda b, h, r: (h, 0, 0)),
            pl.BlockSpec((None, 1, RET_DV), lambda b, h, r: (h, 0, 0)),
        ],
        out_specs=pl.BlockSpec((rows, RET_DV), lambda b, h, r: (row(b, h, r), h)),
        out_shape=jax.ShapeDtypeStruct((TOKENS, RET_V_WIDTH), BF16),
        scratch_shapes=[pltpu.VMEM((RET_DK, RET_DV), F32),
                        pltpu.VMEM((rows, RET_DV), F32),
                        pltpu.VMEM((n_chunks, RET_DK, RET_DV), F32)],
        compiler_params=_params(("parallel", "parallel", "arbitrary")),
        name="retention",
    )(proj, proj, proj, proj, dec, qd, kd, cd)


def _outproj_kernel(y_ref, w_ref, res_ref, o_ref):
    o_ref[...] = res_ref[...] + jnp.dot(y_ref[...], w_ref[...], preferred_element_type=F32)


def _outproj(y, w, res, *, tm=512, tn=1024):
    k = y.shape[1]
    return pl.pallas_call(
        _outproj_kernel,
        grid=(D_MODEL // tn, TOKENS // tm),
        in_specs=[
            pl.BlockSpec((tm, k), lambda j, i: (i, 0)),
            pl.BlockSpec((None, k, tn), lambda j, i: (0, 0, j)),
            pl.BlockSpec((tm, tn), lambda j, i: (i, j)),
        ],
        out_specs=pl.BlockSpec((tm, tn), lambda j, i: (i, j)),
        out_shape=jax.ShapeDtypeStruct((TOKENS, D_MODEL), F32),
        compiler_params=_params(("parallel", "parallel")),
        name="outproj",
    )(y, w, res)


def _mlp_kernel(h_ref, gain_ref, w1_ref, w2_ref, *rest, n_side):
    side_in, o_ref, side_out, hn_ref = (rest[:n_side], rest[n_side],
                                        rest[n_side + 1:2 * n_side + 1], rest[2 * n_side + 1])
    @pl.when(pl.program_id(1) == 0)
    def _():
        h = h_ref[...]
        hn_ref[...] = _rms_norm_bf16(h, gain_ref[...])
        o_ref[...] = h

    _run_side_casts(side_in, side_out)
    a = jnp.dot(hn_ref[...], w1_ref[...], preferred_element_type=F32)
    act = jnp.square(jnp.maximum(a, 0.0)).astype(BF16)
    o_ref[...] += jnp.dot(act, w2_ref[...], preferred_element_type=F32)


def _mlp(h, gain, w1, w2, side_weights=(), *, tm=1024, tf=512):
    n_ff_tiles = D_FF // tf
    side_ops, side_in, side_out, side_shapes = _side_cast(
        side_weights, lambda i, f: i * n_ff_tiles + f)
    return pl.pallas_call(
        functools.partial(_mlp_kernel, n_side=len(side_ops)),
        grid=(TOKENS // tm, n_ff_tiles),
        in_specs=[
            pl.BlockSpec((tm, D_MODEL), lambda i, f: (i, 0)),
            pl.BlockSpec((1, D_MODEL), lambda i, f: (0, 0)),
            pl.BlockSpec((None, D_MODEL, tf), lambda i, f: (0, 0, f)),
            pl.BlockSpec((None, tf, D_MODEL), lambda i, f: (0, f, 0)),
        ] + side_in,
        out_specs=[pl.BlockSpec((tm, D_MODEL), lambda i, f: (i, 0))] + side_out,
        out_shape=[jax.ShapeDtypeStruct((TOKENS, D_MODEL), F32)] + side_shapes,
        scratch_shapes=[pltpu.VMEM((tm, D_MODEL), BF16)],
        compiler_params=_params(("arbitrary", "arbitrary")),
        name="mlp",
    )(h, gain, w1, w2, *side_ops)


ATT_TILE = 256
ATT_SUB = 256
ATT_ROW_PARTS = 2


def _residue_major_matrix(dilation):
    r = jnp.arange(ATT_TILE)
    n = ATT_TILE // dilation
    src = (r % n) * dilation + r // n
    return (src[:, None] == r[None, :]).astype(BF16)


def _head_aux_matrix():
    i = jnp.arange(ATT_SUB)
    row, col = i[:, None], i[None, :]
    same_head = (row // ATT_HEAD_DIM) == (col // ATT_HEAD_DIM)
    r, c = row % ATT_HEAD_DIM, col % ATT_HEAD_DIM
    minus = same_head & (c < ROPE_HALF) & (r == c + ROPE_HALF)
    plus = same_head & (r < ROPE_HALF) & (c == r + ROPE_HALF)
    return (plus.astype(F32) - minus.astype(F32)).astype(BF16)


def _att_inproj_kernel(h_ref, gain_ref, w_ref, qg_ref, kg_ref, cos_ref, sin_ref, aux_ref,
                       p1_ref, p2_ref, o_ref, hn_ref, *, tiles_per_unit, n_q_tiles, n_qk_tiles):
    j = pl.program_id(1)
    tm, tn = o_ref.shape

    @pl.when(j == 0)
    def _():
        hn_ref[0] = _rms_norm_bf16(h_ref[...], gain_ref[...])
        for g, p_ref in ((1, p1_ref), (2, p2_ref)):
            for t in range(tm // ATT_TILE):
                rows = slice(t * ATT_TILE, (t + 1) * ATT_TILE)
                hn_ref[g, rows, :] = jnp.dot(p_ref[...], hn_ref[0, rows, :],
                                             preferred_element_type=F32).astype(BF16)

    g = (j // tiles_per_unit) % N_ATT_GROUPS

    row_part = tm // ATT_ROW_PARTS
    units = [(k, r) for r in range(ATT_ROW_PARTS) for k in range(tn // ATT_SUB)]

    def unit_dot(k, r):
        return jnp.dot(hn_ref[g, r * row_part:(r + 1) * row_part, :],
                       w_ref[:, k * ATT_SUB:(k + 1) * ATT_SUB], preferred_element_type=F32)

    @pl.when(j >= n_qk_tiles)
    def _():
        for k, r in units:
            o_ref[r * row_part:(r + 1) * row_part, k * ATT_SUB:(k + 1) * ATT_SUB] = (
                unit_dot(k, r).astype(o_ref.dtype))

    @pl.when(j < n_qk_tiles)
    def _():
        is_q = j < n_q_tiles
        head_gain = jnp.where(is_q, qg_ref[...] * (ATT_HEAD_DIM ** -0.5 * LOG2_E), kg_ref[...])

        def epilogue(k, r, acc):
            rows = slice(r * row_part, (r + 1) * row_part)
            c = cos_ref[rows, :]
            s = sin_ref[rows, :]
            xg = acc * jnp.concatenate([head_gain, head_gain], axis=1)
            partner = jnp.dot(xg.astype(BF16), aux_ref[...], preferred_element_type=F32)
            for hh in range(ATT_SUB // ATT_HEAD_DIM):
                cols = slice(hh * ATT_HEAD_DIM, (hh + 1) * ATT_HEAD_DIM)
                x = acc[:, cols]
                inv = lax.rsqrt(jnp.mean(x * x, axis=-1, keepdims=True) + EPS)
                rot = (xg[:, cols] * c + partner[:, cols] * s) * inv
                lo = k * ATT_SUB + hh * ATT_HEAD_DIM
                o_ref[rows, lo:lo + ATT_HEAD_DIM] = rot.astype(o_ref.dtype)

        acc = unit_dot(*units[0])
        for n, (k, r) in enumerate(units):
            nxt = unit_dot(*units[n + 1]) if n + 1 < len(units) else None
            epilogue(k, r, acc)
            acc = nxt


def _att_inproj(h, gain, w, q_gain, k_gain, cos, sin, *, tm=1024, tn=1024):
    n_pos_blocks = SEQ // tm
    group_width = N_ATT_GROUPS * ATT_WIDTH
    tiles_per_unit = ATT_WIDTH // tn
    kern = functools.partial(
        _att_inproj_kernel, tiles_per_unit=tiles_per_unit, n_q_tiles=group_width // tn,
        n_qk_tiles=2 * group_width // tn)
    pos_spec = pl.BlockSpec(
        (None, tm, ATT_HEAD_DIM),
        lambda i, j: ((j // tiles_per_unit) % N_ATT_GROUPS, i % n_pos_blocks, 0))
    return pl.pallas_call(
        kern,
        grid=(TOKENS // tm, ATT_IN_WIDTH // tn),
        in_specs=[
            pl.BlockSpec((tm, D_MODEL), lambda i, j: (i, 0)),
            pl.BlockSpec((1, D_MODEL), lambda i, j: (0, 0)),
            pl.BlockSpec((None, D_MODEL, tn), lambda i, j: (0, 0, j)),
            pl.BlockSpec((1, ATT_HEAD_DIM), lambda i, j: (0, 0)),
            pl.BlockSpec((1, ATT_HEAD_DIM), lambda i, j: (0, 0)),
            pos_spec, pos_spec,
            pl.BlockSpec((2 * ATT_HEAD_DIM, 2 * ATT_HEAD_DIM), lambda i, j: (0, 0)),
            pl.BlockSpec((ATT_TILE, ATT_TILE), lambda i, j: (0, 0)),
            pl.BlockSpec((ATT_TILE, ATT_TILE), lambda i, j: (0, 0)),
        ],
        out_specs=pl.BlockSpec((tm, tn), lambda i, j: (i, j)),
        out_shape=jax.ShapeDtypeStruct((TOKENS, ATT_IN_WIDTH), BF16),
        scratch_shapes=[pltpu.VMEM((N_ATT_GROUPS, tm, D_MODEL), BF16)],
        compiler_params=_params(("parallel", "arbitrary")),
        name="att_inproj",
    )(h, gain, w, q_gain, k_gain, cos, sin, _head_aux_matrix(),
      _residue_major_matrix(ATT_GROUPS[1][1]), _residue_major_matrix(ATT_GROUPS[2][1]))


ATT_STEP = 4
LOG2_E = 1.4426950408889634
LN_2 = 0.6931471805599453


def _attention_kernel(q_ref, k_ref, v_ref, o_ref, lse_ref, kp_ref, vp_ref, s_ref):
    n = pl.program_id(2)

    @pl.when(n == 0)
    def _():
        kp_ref[...] = jnp.zeros_like(kp_ref)
        vp_ref[...] = jnp.zeros_like(vp_ref)

    def block_rows(ref, b):
        if len(ref.shape) == 2:
            return (slice(b * ATT_BLOCK, (b + 1) * ATT_BLOCK),), (ATT_BLOCK,)
        n_sub = ref.shape[1]
        if n_sub >= ATT_BLOCK:
            tile, off = divmod(b * ATT_BLOCK, n_sub)
            return (tile, slice(off, off + ATT_BLOCK)), (ATT_BLOCK,)
        per = ATT_BLOCK // n_sub
        return (slice(b * per, (b + 1) * per), slice(None)), (per, n_sub)

    def load(ref, b, cols):
        idx, _ = block_rows(ref, b)
        x = ref[idx + (cols,)]
        return x.reshape(ATT_BLOCK, x.shape[-1])

    def store(ref, b, cols, value):
        idx, lead = block_rows(ref, b)
        ref[idx + (cols,)] = value.reshape(lead + (value.shape[-1],))

    qi = lax.broadcasted_iota(jnp.int32, (ATT_BLOCK, 2 * ATT_BLOCK), 0)
    kj = lax.broadcasted_iota(jnp.int32, (ATT_BLOCK, 2 * ATT_BLOCK), 1)
    band = (kj >= qi) & (kj <= qi + ATT_BLOCK)
    first_band = band & ((n > 0) | (kj >= ATT_BLOCK))
    lane = lax.broadcasted_iota(jnp.int32, (ATT_BLOCK, LANES), 1)
    contract_last = (((1,), (1,)), ((), ()))
    head_cols = [slice(h * ATT_HEAD_DIM, (h + 1) * ATT_HEAD_DIM) for h in range(ATT_HEADS)]

    for b in range(ATT_STEP):
        for h, cols in enumerate(head_cols):
            q = load(q_ref, b, cols)
            k_prev = kp_ref[:, cols] if b == 0 else load(k_ref, b - 1, cols)
            s_ref[b * ATT_HEADS + h, :, :ATT_BLOCK] = lax.dot_general(
                q, k_prev, contract_last, preferred_element_type=F32)
            s_ref[b * ATT_HEADS + h, :, ATT_BLOCK:] = lax.dot_general(
                q, load(k_ref, b, cols), contract_last, preferred_element_type=F32)

    for b in range(ATT_STEP):
        valid = first_band if b == 0 else band
        max_tile = jnp.zeros((ATT_BLOCK, LANES), F32)
        sum_tile = jnp.ones((ATT_BLOCK, LANES), F32)
        for h, cols in enumerate(head_cols):
            s = jnp.where(valid, s_ref[b * ATT_HEADS + h], -jnp.inf)
            m = jnp.max(s, axis=-1, keepdims=True)
            p = jnp.exp2(s - m)
            denom = jnp.sum(p, axis=-1, keepdims=True)
            pb = p.astype(BF16)
            v_prev = vp_ref[:, cols] if b == 0 else load(v_ref, b - 1, cols)
            pv = (jnp.dot(pb[:, :ATT_BLOCK], v_prev, preferred_element_type=F32)
                  + jnp.dot(pb[:, ATT_BLOCK:], load(v_ref, b, cols), preferred_element_type=F32))
            store(o_ref, b, cols, (pv / denom).astype(o_ref.dtype))
            max_tile = jnp.where(lane == h, m, max_tile)
            sum_tile = jnp.where(lane == h, denom, sum_tile)
        store(lse_ref, b, slice(None), (max_tile + jnp.log2(sum_tile)) * LN_2)

    last = ATT_STEP - 1
    kp_ref[...] = load(k_ref, last, slice(None))
    vp_ref[...] = load(v_ref, last, slice(None))


def _attention_group(qkv, group, dilation):
    n_sub = ATT_TILE // dilation
    step = ATT_STEP * ATT_BLOCK
    n_steps = SEQ // dilation // step
    n_tiles = TOKENS // ATT_TILE
    tiles_per_batch = SEQ // ATT_TILE
    q_unit, k_unit, v_unit = group, N_ATT_GROUPS + group, 2 * N_ATT_GROUPS + group
    view = qkv.reshape(n_tiles, dilation, n_sub, ATT_IN_WIDTH)

    if n_sub >= step:
        per_tile = n_sub // step
        lead = (None, None, step)
        index = lambda b, c, n: (b * tiles_per_batch + n // per_tile, c, n % per_tile)
    else:
        span = step // n_sub
        lead = (span, None, n_sub)
        index = lambda b, c, n: (b * (tiles_per_batch // span) + n, c, 0)

    def spec(width, unit):
        return pl.BlockSpec(lead + (width,), lambda b, c, n: index(b, c, n) + (unit,))

    return pl.pallas_call(
        _attention_kernel,
        grid=(BATCH, dilation, n_steps),
        in_specs=[spec(ATT_WIDTH, q_unit), spec(ATT_WIDTH, k_unit), spec(ATT_WIDTH, v_unit)],
        out_specs=[spec(ATT_WIDTH, 0), spec(LANES, 0)],
        out_shape=[
            jax.ShapeDtypeStruct((n_tiles, dilation, n_sub, ATT_WIDTH), BF16),
            jax.ShapeDtypeStruct((n_tiles, dilation, n_sub, LANES), F32),
        ],
        scratch_shapes=[
            pltpu.VMEM((ATT_BLOCK, ATT_WIDTH), BF16),
            pltpu.VMEM((ATT_BLOCK, ATT_WIDTH), BF16),
            pltpu.VMEM((ATT_STEP * ATT_HEADS, ATT_BLOCK, 2 * ATT_BLOCK), F32),
        ],
        compiler_params=_params(("parallel", "parallel", "arbitrary")),
        name=f"attention_d{dilation}",
    )(view, view, view)


OUT_TM = 256


def _att_outproj_kernel(o0_ref, o1_ref, o2_ref, l0_ref, l1_ref, l2_ref, p1_ref, p2_ref, w_ref,
                        res_ref, out_ref, lse_ref, even_ref, odd_ref):
    s = pl.program_id(0)

    @pl.when(s == 0)
    def _():
        odd_ref[...] = jnp.zeros_like(odd_ref)

    def step(merged_prev_ref, merged_ref):
        out_ref[...] = res_ref[...] + jnp.dot(merged_prev_ref[...], w_ref[...],
                                              preferred_element_type=F32)
        for k, l_ref in enumerate((l1_ref, l2_ref)):
            d, n = l_ref.shape[0], l_ref.shape[1]
            for c in range(d):
                lse_ref[k, pl.ds(c, n, stride=d), :] = l_ref[c]
        l0, l1, l2 = l0_ref[0], lse_ref[0], lse_ref[1]
        top = jnp.maximum(jnp.maximum(l0, l1), l2)
        e0, e1, e2 = jnp.exp(l0 - top), jnp.exp(l1 - top), jnp.exp(l2 - top)
        total = e0 + e1 + e2
        w0, w1, w2 = e0 / total, e1 / total, e2 / total
        o1 = jnp.dot(p1_ref[...], o1_ref[...].reshape(OUT_TM, ATT_WIDTH),
                     preferred_element_type=F32)
        o2 = jnp.dot(p2_ref[...], o2_ref[...].reshape(OUT_TM, ATT_WIDTH),
                     preferred_element_type=F32)
        for h in range(ATT_HEADS):
            cols = slice(h * ATT_HEAD_DIM, (h + 1) * ATT_HEAD_DIM)
            merged = (w0[:, h:h + 1] * o0_ref[0, :, cols].astype(F32)
                      + w1[:, h:h + 1] * o1[:, cols]
                      + w2[:, h:h + 1] * o2[:, cols])
            merged_ref[:, cols] = merged.astype(BF16)

    @pl.when(s % 2 == 0)
    def _():
        step(odd_ref, even_ref)

    @pl.when(s % 2 == 1)
    def _():
        step(even_ref, odd_ref)


def _token_order_matrix(dilation):
    t = jnp.arange(OUT_TM)
    src = (t % dilation) * (OUT_TM // dilation) + t // dilation
    return (src[:, None] == t[None, :]).astype(BF16)


def _att_outproj(outs, lses, w, res):
    tm = OUT_TM
    per_tile = ATT_TILE // tm

    n_row_tiles = TOKENS // tm
    merge_tile = lambda s: jnp.minimum(s, n_row_tiles - 1)
    project_tile = lambda s: jnp.maximum(s - 1, 0)

    def spec(dilation, width):
        return pl.BlockSpec(
            (None, dilation, tm // dilation, width),
            lambda s: (merge_tile(s) // per_tile, 0, merge_tile(s) % per_tile, 0))

    dils = [d for _, d in ATT_GROUPS]
    const = lambda shape: pl.BlockSpec(shape, lambda s: (0, 0))
    return pl.pallas_call(
        _att_outproj_kernel,
        grid=(n_row_tiles + 1,),
        in_specs=[spec(d, ATT_WIDTH) for d in dils] + [spec(d, LANES) for d in dils] + [
            const((tm, tm)), const((tm, tm)),
            pl.BlockSpec((None, ATT_WIDTH, D_MODEL), lambda s: (0, 0, 0)),
            pl.BlockSpec((tm, D_MODEL), lambda s: (project_tile(s), 0))],
        out_specs=pl.BlockSpec((tm, D_MODEL), lambda s: (project_tile(s), 0)),
        out_shape=jax.ShapeDtypeStruct((TOKENS, D_MODEL), F32),
        scratch_shapes=[pltpu.VMEM((N_ATT_GROUPS - 1, tm, LANES), F32),
                        pltpu.VMEM((tm, ATT_WIDTH), BF16),
                        pltpu.VMEM((tm, ATT_WIDTH), BF16)],
        compiler_params=_params(("arbitrary",)),
        name="att_outproj",
    )(*outs, *lses, _token_order_matrix(dils[1]), _token_order_matrix(dils[2]), w, res)


def _cos_sin(pos_lo, inv_freq):
    pos_hi = jnp.arange(SEQ // ATT_TILE, dtype=F32) * ATT_TILE
    ang_hi = pos_hi[:, None] * inv_freq
    ang_lo = pos_lo[..., None] * inv_freq
    ch, sh, cl, sl = lax.optimization_barrier(
        (jnp.cos(ang_hi), jnp.sin(ang_hi), jnp.cos(ang_lo), jnp.sin(ang_lo)))
    ch, sh = ch[:, None, :], sh[:, None, :]
    cl, sl = cl[..., None, :, :], sl[..., None, :, :]
    shape = pos_lo.shape[:-1] + (SEQ, inv_freq.shape[0])
    return (ch * cl - sh * sl).reshape(shape), (sh * cl + ch * sl).reshape(shape)


def _retention_tables():
    inv_freq = 1.0 / (RET_ROT_BASE ** jnp.linspace(0.0, 1.0, RET_DK // 2, dtype=F32))
    cos, sin = _cos_sin(jnp.arange(ATT_TILE, dtype=F32), inv_freq)
    log_gamma = jnp.log(1.0 - 2.0 ** (-5.0 - jnp.arange(RET_HEADS, dtype=F32)))
    idx = jnp.arange(RET_BLOCK, dtype=F32)
    pow_i, pow_neg_j = lax.optimization_barrier(
        (jnp.exp(log_gamma[:, None] * idx[None, :]), jnp.exp(-log_gamma[:, None] * idx[None, :])))
    inner_decay = jnp.where(idx[:, None] >= idx[None, :],
                            pow_i[:, :, None] * pow_neg_j[:, None, :], 0.0)
    query_decay = jnp.exp(log_gamma[:, None] * (idx[None, :] + 1.0))[:, :, None]
    key_decay = jnp.exp(log_gamma[:, None] * (RET_BLOCK - 1.0 - idx[None, :]))[:, :, None]
    chunk_decay = jnp.broadcast_to(jnp.exp(log_gamma * RET_BLOCK)[:, None, None],
                                   (RET_HEADS, 1, RET_DV))
    return cos, sin, inner_decay, query_decay, key_decay, chunk_decay


def _attention_tables():
    pos_lo = jnp.arange(ATT_TILE, dtype=F32)
    pos_lo = jnp.stack([pos_lo.reshape(ATT_TILE // d, d).T.reshape(ATT_TILE)
                        for _, d in ATT_GROUPS])
    inv_freq = ROPE_THETA ** (-jnp.arange(0, ROPE_DIM, 2, dtype=F32) / ROPE_DIM)
    inv_freq = jnp.concatenate([inv_freq, inv_freq, jnp.zeros((ATT_HEAD_DIM - ROPE_DIM,), F32)])
    return _cos_sin(pos_lo, inv_freq)


def kernel(x, norm_mix_gain, norm_mlp_gain, ret_w_in, ret_w_out, att_w_in, att_q_gain,
           att_k_gain, att_w_out, mlp_w_in, mlp_w_out):
    h = x.reshape(TOKENS, D_MODEL)

    cos, sin, inner_decay, query_decay, key_decay, chunk_decay = _retention_tables()
    proj, ret_wo, mlp0_w1, mlp0_w2 = _ret_inproj(
        h, norm_mix_gain[0:1], ret_w_in.astype(BF16), cos, sin,
        [(ret_w_out, 0), (mlp_w_in, 0), (mlp_w_out, 0)])
    y = _retention(proj, inner_decay, query_decay, key_decay, chunk_decay)
    h = _outproj(y, ret_wo, h)
    h, att_wi, att_wo, mlp1_w1, mlp1_w2 = _mlp(
        h, norm_mlp_gain[0:1], mlp0_w1, mlp0_w2,
        [(att_w_in, 0), (att_w_out, 0), (mlp_w_in, 1), (mlp_w_out, 1)])

    cos_full, sin_full = _attention_tables()
    qkv = _att_inproj(h, norm_mix_gain[1:2], att_wi, att_q_gain[0:1], att_k_gain[0:1],
                      cos_full, sin_full)
    outs, lses = [], []
    for g, (_, dilation) in enumerate(ATT_GROUPS):
        o, l = _attention_group(qkv, g, dilation)
        outs.append(o)
        lses.append(l)
    h = _att_outproj(outs, lses, att_wo, h)
    (h,) = _mlp(h, norm_mlp_gain[1:2], mlp1_w1, mlp1_w2)
    return h.reshape(BATCH, SEQ, D_MODEL)
```

```python
import functools

import jax
import jax.numpy as jnp
from jax import lax
from jax.experimental import pallas as pl
from jax.experimental.pallas import tpu as pltpu

F32 = jnp.float32
BF16 = jnp.bfloat16

D_MODEL = 2048
BATCH = 2
SEQ = 8192
TOKENS = BATCH * SEQ
EPS = 1e-6

RET_HEADS = 8
RET_DK = 256
RET_DV = 512
RET_QK_WIDTH = RET_HEADS * RET_DK
RET_V_WIDTH = RET_HEADS * RET_DV
RET_IN_WIDTH = 2 * RET_QK_WIDTH + 2 * RET_V_WIDTH
RET_CHUNK = 128
RET_ROT_BASE = 10000.0

ATT_GROUPS = ((128, 1), (512, 4), (2048, 16))
N_ATT_GROUPS = 3
ATT_HEAD_DIM = 128
ATT_HEADS = 16
ATT_WIDTH = ATT_HEADS * ATT_HEAD_DIM
ATT_IN_WIDTH = 3 * N_ATT_GROUPS * ATT_WIDTH
ATT_BLOCK = 128
ROPE_DIM = 32
ROPE_HALF = ROPE_DIM // 2
ROPE_THETA = 500000.0

D_FF = 4 * D_MODEL

LANES = 128
VMEM_LIMIT = 60 * 1024 * 1024


def _params(semantics):
    return pltpu.CompilerParams(dimension_semantics=semantics, vmem_limit_bytes=VMEM_LIMIT)


def _rms_norm_bf16(x, gain):
    ms = jnp.mean(x * x, axis=-1, keepdims=True)
    return (x * lax.rsqrt(ms + EPS) * gain).astype(BF16)


def _side_cast(weights, step_of, n_steps):
    bf16_sublanes = 16
    n_blocks = 1 << (n_steps.bit_length() - 1)
    for arr, _ in weights:
        n_blocks = min(n_blocks, arr.shape[1] // bf16_sublanes)
    operands, in_specs, out_specs, out_shapes = [], [], [], []
    for arr, layer in weights:
        _, rows, cols = arr.shape
        assert rows % (n_blocks * bf16_sublanes) == 0, (rows, n_blocks)
        block = (None, rows // n_blocks, cols)
        blk = lambda *ids: jnp.minimum(step_of(*ids), n_blocks - 1)
        operands.append(arr)
        in_specs.append(pl.BlockSpec(block, lambda *ids, layer=layer, blk=blk: (layer, blk(*ids), 0)))
        out_specs.append(pl.BlockSpec(block, lambda *ids, blk=blk: (0, blk(*ids), 0)))
        out_shapes.append(jax.ShapeDtypeStruct((1, rows, cols), BF16))
    return operands, in_specs, out_specs, out_shapes


def _run_side_casts(src_refs, dst_refs):
    for src, dst in zip(src_refs, dst_refs):
        dst[...] = src[...].astype(BF16)


RET_ROW_PARTS = 2


def _ret_inproj_kernel(h_ref, gain_ref, w_ref, cos_ref, sin_ref, *rest,
                       n_side, n_q_tiles, n_rot_tiles, n_plain_tiles, heads_per_tile):
    side_in, o_ref, side_out, hn_ref = (rest[:n_side], rest[n_side],
                                        rest[n_side + 1:2 * n_side + 1], rest[2 * n_side + 1])
    j = pl.program_id(1)

    @pl.when(j == 0)
    def _():
        hn_ref[...] = _rms_norm_bf16(h_ref[...], gain_ref[...])

    row_part = o_ref.shape[0] // RET_ROW_PARTS
    units = [(hh, r) for hh in range(heads_per_tile) for r in range(RET_ROW_PARTS)]

    def unit_dot(hh, r):
        return jnp.dot(hn_ref[r * row_part:(r + 1) * row_part, :],
                       w_ref[:, hh * RET_DK:(hh + 1) * RET_DK], preferred_element_type=F32)

    @pl.when((j >= n_rot_tiles) & (j < n_plain_tiles))
    def _():
        _run_side_casts(side_in, side_out)
        for hh, r in units:
            o_ref[r * row_part:(r + 1) * row_part, hh * RET_DK:(hh + 1) * RET_DK] = (
                unit_dot(hh, r).astype(o_ref.dtype))

    @pl.when(j >= n_plain_tiles)
    def _():
        _run_side_casts(side_in, side_out)
        for hh, r in units:
            gate = unit_dot(hh, r)
            half_gate = 0.5 * gate
            o_ref[r * row_part:(r + 1) * row_part, hh * RET_DK:(hh + 1) * RET_DK] = (
                half_gate + half_gate * jnp.tanh(half_gate)).astype(o_ref.dtype)

    @pl.when(j < n_rot_tiles)
    def _():
        _run_side_casts(side_in, side_out)
        scale = jnp.where(j >= n_q_tiles, RET_DK ** -0.5, 1.0).astype(F32)
        half = RET_DK // 2
        for hh, r in units:
            rows = slice(r * row_part, (r + 1) * row_part)
            c = cos_ref[rows, :] * scale
            s = sin_ref[rows, :] * scale
            acc = unit_dot(hh, r)
            lo = hh * RET_DK
            x1 = acc[:, :half]
            x2 = acc[:, half:]
            o_ref[rows, lo:lo + half] = (x1 * c - x2 * s).astype(o_ref.dtype)
            o_ref[rows, lo + half:lo + RET_DK] = (x2 * c + x1 * s).astype(o_ref.dtype)


def _ret_inproj(h, gain, w, cos, sin, side_weights, *, tm=1024, tn=2048):
    n_pos_blocks = SEQ // tm
    n_col_tiles = RET_IN_WIDTH // tn
    side_ops, side_in, side_out, side_shapes = _side_cast(
        side_weights, lambda i, j: i * n_col_tiles + j, (TOKENS // tm) * n_col_tiles)
    kern = functools.partial(
        _ret_inproj_kernel, n_side=len(side_ops), n_q_tiles=RET_QK_WIDTH // tn,
        n_rot_tiles=2 * RET_QK_WIDTH // tn,
        n_plain_tiles=(2 * RET_QK_WIDTH + RET_V_WIDTH) // tn, heads_per_tile=tn // RET_DK)
    return pl.pallas_call(
        kern,
        grid=(TOKENS // tm, n_col_tiles),
        in_specs=[
            pl.BlockSpec((tm, D_MODEL), lambda i, j: (i, 0)),
            pl.BlockSpec((1, D_MODEL), lambda i, j: (0, 0)),
            pl.BlockSpec((None, D_MODEL, tn), lambda i, j: (0, 0, j)),
            pl.BlockSpec((tm, RET_DK // 2), lambda i, j: (i % n_pos_blocks, 0)),
            pl.BlockSpec((tm, RET_DK // 2), lambda i, j: (i % n_pos_blocks, 0)),
        ] + side_in,
        out_specs=[pl.BlockSpec((tm, tn), lambda i, j: (i, j))] + side_out,
        out_shape=[jax.ShapeDtypeStruct((TOKENS, RET_IN_WIDTH), BF16)] + side_shapes,
        scratch_shapes=[pltpu.VMEM((tm, D_MODEL), BF16)],
        compiler_params=_params(("arbitrary", "arbitrary")),
        name="ret_inproj",
    )(h, gain, w, cos, sin, *side_ops)


RET_BLOCK = 256


def _retention_kernel(q_ref, k_ref, v_ref, g_ref, dec_ref, qd_ref, kd_ref, cd_ref, o_ref,
                      state_ref, acc_ref, upd_ref, *, n_chunks):
    @pl.when(pl.program_id(2) == 0)
    def _():
        state_ref[...] = jnp.zeros_like(state_ref)

    dec = dec_ref[...]
    qd = qd_ref[...]
    kd = kd_ref[...]
    cd = cd_ref[...]
    chunk = lambda c: slice(c * RET_BLOCK, (c + 1) * RET_BLOCK)

    for c in range(n_chunks):
        qc, kc, vc = q_ref[chunk(c), :], k_ref[chunk(c), :], v_ref[chunk(c), :]
        scores = lax.dot_general(qc, kc, (((1,), (1,)), ((), ())),
                                 preferred_element_type=F32) * dec
        acc_ref[chunk(c), :] = jnp.dot(scores.astype(BF16), vc, preferred_element_type=F32)
        k_dec = (kc.astype(F32) * kd).astype(BF16)
        upd_ref[c] = lax.dot_general(k_dec, vc, (((0,), (0,)), ((), ())),
                                     preferred_element_type=F32)

    for c in range(n_chunks):
        state = state_ref[...]
        cross = jnp.dot(q_ref[chunk(c), :], state.astype(BF16), preferred_element_type=F32) * qd
        state_ref[...] = state * cd + upd_ref[c]
        out = acc_ref[chunk(c), :] + cross
        ms = jnp.mean(out * out, axis=-1, keepdims=True)
        y = out * lax.rsqrt(ms + EPS) * g_ref[chunk(c), :].astype(F32)
        o_ref[chunk(c), :] = y.astype(o_ref.dtype)


def _retention(proj, dec, qd, kd, cd, *, rows=1024):
    n_row_blocks = SEQ // rows
    n_chunks = rows // RET_BLOCK
    k_off = RET_QK_WIDTH // RET_DK
    v_off = 2 * RET_QK_WIDTH // RET_DV
    g_off = v_off + RET_V_WIDTH // RET_DV
    row = lambda b, h, r: b * n_row_blocks + r
    kern = functools.partial(_retention_kernel, n_chunks=n_chunks)
    return pl.pallas_call(
        kern,
        grid=(BATCH, RET_HEADS, n_row_blocks),
        in_specs=[
            pl.BlockSpec((rows, RET_DK), lambda b, h, r: (row(b, h, r), h)),
            pl.BlockSpec((rows, RET_DK), lambda b, h, r: (row(b, h, r), k_off + h)),
            pl.BlockSpec((rows, RET_DV), lambda b, h, r: (row(b, h, r), v_off + h)),
            pl.BlockSpec((rows, RET_DV), lambda b, h, r: (row(b, h, r), g_off + h)),
            pl.BlockSpec((None, RET_BLOCK, RET_BLOCK), lambda b, h, r: (h, 0, 0)),
            pl.BlockSpec((None, RET_BLOCK, 1), lambda b, h, r: (h, 0, 0)),
            pl.BlockSpec((None, RET_BLOCK, 1), lambda b, h, r: (h, 0, 0)),
            pl.BlockSpec((None, 1, RET_DV), lambda b, h, r: (h, 0, 0)),
        ],
        out_specs=pl.BlockSpec((rows, RET_DV), lambda b, h, r: (row(b, h, r), h)),
        out_shape=jax.ShapeDtypeStruct((TOKENS, RET_V_WIDTH), BF16),
        scratch_shapes=[pltpu.VMEM((RET_DK, RET_DV), F32),
                        pltpu.VMEM((rows, RET_DV), F32),
                        pltpu.VMEM((n_chunks, RET_DK, RET_DV), F32)],
        compiler_params=_params(("parallel", "parallel", "arbitrary")),
        name="retention",
    )(proj, proj, proj, proj, dec, qd, kd, cd)


def _outproj_kernel(y_ref, w_ref, res_ref, o_ref):
    o_ref[...] = res_ref[...] + jnp.dot(y_ref[...], w_ref[...], preferred_element_type=F32)


def _outproj(y, w, res, *, tm=512, tn=1024):
    k = y.shape[1]
    return pl.pallas_call(
        _outproj_kernel,
        grid=(D_MODEL // tn, TOKENS // tm),
        in_specs=[
            pl.BlockSpec((tm, k), lambda j, i: (i, 0)),
            pl.BlockSpec((None, k, tn), lambda j, i: (0, 0, j)),
            pl.BlockSpec((tm, tn), lambda j, i: (i, j)),
        ],
        out_specs=pl.BlockSpec((tm, tn), lambda j, i: (i, j)),
        out_shape=jax.ShapeDtypeStruct((TOKENS, D_MODEL), F32),
        compiler_params=_params(("parallel", "parallel")),
        name="outproj",
    )(y, w, res)


def _mlp_kernel(h_ref, gain_ref, w1_ref, w2_ref, *rest, n_side):
    side_in, o_ref, side_out, hn_ref = (rest[:n_side], rest[n_side],
                                        rest[n_side + 1:2 * n_side + 1], rest[2 * n_side + 1])
    @pl.when(pl.program_id(1) == 0)
    def _():
        h = h_ref[...]
        hn_ref[...] = _rms_norm_bf16(h, gain_ref[...])
        o_ref[...] = h

    _run_side_casts(side_in, side_out)
    a = jnp.dot(hn_ref[...], w1_ref[...], preferred_element_type=F32)
    act = jnp.square(jnp.maximum(a, 0.0)).astype(BF16)
    o_ref[...] += jnp.dot(act, w2_ref[...], preferred_element_type=F32)


def _mlp(h, gain, w1, w2, side_weights=(), *, tm=1024, tf=512):
    n_ff_tiles = D_FF // tf
    side_ops, side_in, side_out, side_shapes = _side_cast(
        side_weights, lambda i, f: i * n_ff_tiles + f, (TOKENS // tm) * n_ff_tiles)
    return pl.pallas_call(
        functools.partial(_mlp_kernel, n_side=len(side_ops)),
        grid=(TOKENS // tm, n_ff_tiles),
        in_specs=[
            pl.BlockSpec((tm, D_MODEL), lambda i, f: (i, 0)),
            pl.BlockSpec((1, D_MODEL), lambda i, f: (0, 0)),
            pl.BlockSpec((None, D_MODEL, tf), lambda i, f: (0, 0, f)),
            pl.BlockSpec((None, tf, D_MODEL), lambda i, f: (0, f, 0)),
        ] + side_in,
        out_specs=[pl.BlockSpec((tm, D_MODEL), lambda i, f: (i, 0))] + side_out,
        out_shape=[jax.ShapeDtypeStruct((TOKENS, D_MODEL), F32)] + side_shapes,
        scratch_shapes=[pltpu.VMEM((tm, D_MODEL), BF16)],
        compiler_params=_params(("arbitrary", "arbitrary")),
        name="mlp",
    )(h, gain, w1, w2, *side_ops)


ATT_TILE = 256
ATT_SUB = 256
ATT_ROW_PARTS = 2


def _residue_major_matrix(dilation):
    r = jnp.arange(ATT_TILE)
    n = ATT_TILE // dilation
    src = (r % n) * dilation + r // n
    return (src[:, None] == r[None, :]).astype(BF16)


def _head_aux_matrix():
    i = jnp.arange(ATT_SUB)
    row, col = i[:, None], i[None, :]
    same_head = (row // ATT_HEAD_DIM) == (col // ATT_HEAD_DIM)
    r, c = row % ATT_HEAD_DIM, col % ATT_HEAD_DIM
    minus = same_head & (c < ROPE_HALF) & (r == c + ROPE_HALF)
    plus = same_head & (r < ROPE_HALF) & (c == r + ROPE_HALF)
    return (plus.astype(F32) - minus.astype(F32)).astype(BF16)


def _att_inproj_kernel(h_ref, gain_ref, w_ref, qg_ref, kg_ref, cos_ref, sin_ref, aux_ref,
                       p1_ref, p2_ref, o_ref, hn_ref, *, tiles_per_unit, n_q_tiles, n_qk_tiles):
    j = pl.program_id(1)
    tm, tn = o_ref.shape

    @pl.when(j == 0)
    def _():
        hn_ref[0] = _rms_norm_bf16(h_ref[...], gain_ref[...])
        for g, p_ref in ((1, p1_ref), (2, p2_ref)):
            for t in range(tm // ATT_TILE):
                rows = slice(t * ATT_TILE, (t + 1) * ATT_TILE)
                hn_ref[g, rows, :] = jnp.dot(p_ref[...], hn_ref[0, rows, :],
                                             preferred_element_type=F32).astype(BF16)

    g = (j // tiles_per_unit) % N_ATT_GROUPS

    row_part = tm // ATT_ROW_PARTS
    units = [(k, r) for r in range(ATT_ROW_PARTS) for k in range(tn // ATT_SUB)]

    def unit_dot(k, r):
        return jnp.dot(hn_ref[g, r * row_part:(r + 1) * row_part, :],
                       w_ref[:, k * ATT_SUB:(k + 1) * ATT_SUB], preferred_element_type=F32)

    @pl.when(j >= n_qk_tiles)
    def _():
        for k, r in units:
            o_ref[r * row_part:(r + 1) * row_part, k * ATT_SUB:(k + 1) * ATT_SUB] = (
                unit_dot(k, r).astype(o_ref.dtype))

    @pl.when(j < n_qk_tiles)
    def _():
        is_q = j < n_q_tiles
        head_gain = jnp.where(is_q, qg_ref[...] * (ATT_HEAD_DIM ** -0.5 * LOG2_E), kg_ref[...])

        def epilogue(k, r, acc):
            rows = slice(r * row_part, (r + 1) * row_part)
            c = cos_ref[rows, :]
            s = sin_ref[rows, :]
            xg = acc * jnp.concatenate([head_gain, head_gain], axis=1)
            partner = jnp.dot(xg.astype(BF16), aux_ref[...], preferred_element_type=F32)
            for hh in range(ATT_SUB // ATT_HEAD_DIM):
                cols = slice(hh * ATT_HEAD_DIM, (hh + 1) * ATT_HEAD_DIM)
                x = acc[:, cols]
                inv = lax.rsqrt(jnp.mean(x * x, axis=-1, keepdims=True) + EPS)
                rot = (xg[:, cols] * c + partner[:, cols] * s) * inv
                lo = k * ATT_SUB + hh * ATT_HEAD_DIM
                o_ref[rows, lo:lo + ATT_HEAD_DIM] = rot.astype(o_ref.dtype)

        acc = unit_dot(*units[0])
        for n, (k, r) in enumerate(units):
            nxt = unit_dot(*units[n + 1]) if n + 1 < len(units) else None
            epilogue(k, r, acc)
            acc = nxt


def _att_inproj(h, gain, w, q_gain, k_gain, cos, sin, *, tm=1024, tn=2048):
    n_pos_blocks = SEQ // tm
    group_width = N_ATT_GROUPS * ATT_WIDTH
    tiles_per_unit = ATT_WIDTH // tn
    kern = functools.partial(
        _att_inproj_kernel, tiles_per_unit=tiles_per_unit, n_q_tiles=group_width // tn,
        n_qk_tiles=2 * group_width // tn)
    pos_spec = pl.BlockSpec(
        (None, tm, ATT_HEAD_DIM),
        lambda i, j: ((j // tiles_per_unit) % N_ATT_GROUPS, i % n_pos_blocks, 0))
    return pl.pallas_call(
        kern,
        grid=(TOKENS // tm, ATT_IN_WIDTH // tn),
        in_specs=[
            pl.BlockSpec((tm, D_MODEL), lambda i, j: (i, 0)),
            pl.BlockSpec((1, D_MODEL), lambda i, j: (0, 0)),
            pl.BlockSpec((None, D_MODEL, tn), lambda i, j: (0, 0, j)),
            pl.BlockSpec((1, ATT_HEAD_DIM), lambda i, j: (0, 0)),
            pl.BlockSpec((1, ATT_HEAD_DIM), lambda i, j: (0, 0)),
            pos_spec, pos_spec,
            pl.BlockSpec((2 * ATT_HEAD_DIM, 2 * ATT_HEAD_DIM), lambda i, j: (0, 0)),
            pl.BlockSpec((ATT_TILE, ATT_TILE), lambda i, j: (0, 0)),
            pl.BlockSpec((ATT_TILE, ATT_TILE), lambda i, j: (0, 0)),
        ],
        out_specs=pl.BlockSpec((tm, tn), lambda i, j: (i, j)),
        out_shape=jax.ShapeDtypeStruct((TOKENS, ATT_IN_WIDTH), BF16),
        scratch_shapes=[pltpu.VMEM((N_ATT_GROUPS, tm, D_MODEL), BF16)],
        compiler_params=_params(("parallel", "arbitrary")),
        name="att_inproj",
    )(h, gain, w, q_gain, k_gain, cos, sin, _head_aux_matrix(),
      _residue_major_matrix(ATT_GROUPS[1][1]), _residue_major_matrix(ATT_GROUPS[2][1]))


ATT_STEP = 4
LOG2_E = 1.4426950408889634
LN_2 = 0.6931471805599453


def _attention_kernel(q_ref, k_ref, v_ref, o_ref, lse_ref, kp_ref, vp_ref, s_ref):
    n = pl.program_id(2)

    @pl.when(n == 0)
    def _():
        kp_ref[...] = jnp.zeros_like(kp_ref)
        vp_ref[...] = jnp.zeros_like(vp_ref)

    def block_rows(ref, b):
        if len(ref.shape) == 2:
            return (slice(b * ATT_BLOCK, (b + 1) * ATT_BLOCK),), (ATT_BLOCK,)
        n_sub = ref.shape[1]
        if n_sub >= ATT_BLOCK:
            tile, off = divmod(b * ATT_BLOCK, n_sub)
            return (tile, slice(off, off + ATT_BLOCK)), (ATT_BLOCK,)
        per = ATT_BLOCK // n_sub
        return (slice(b * per, (b + 1) * per), slice(None)), (per, n_sub)

    def load(ref, b, cols):
        idx, _ = block_rows(ref, b)
        x = ref[idx + (cols,)]
        return x.reshape(ATT_BLOCK, x.shape[-1])

    def store(ref, b, cols, value):
        idx, lead = block_rows(ref, b)
        ref[idx + (cols,)] = value.reshape(lead + (value.shape[-1],))

    qi = lax.broadcasted_iota(jnp.int32, (ATT_BLOCK, 2 * ATT_BLOCK), 0)
    kj = lax.broadcasted_iota(jnp.int32, (ATT_BLOCK, 2 * ATT_BLOCK), 1)
    band = (kj >= qi) & (kj <= qi + ATT_BLOCK)
    first_band = band & ((n > 0) | (kj >= ATT_BLOCK))
    lane = lax.broadcasted_iota(jnp.int32, (ATT_BLOCK, LANES), 1)
    contract_last = (((1,), (1,)), ((), ()))
    head_cols = [slice(h * ATT_HEAD_DIM, (h + 1) * ATT_HEAD_DIM) for h in range(ATT_HEADS)]

    for b in range(ATT_STEP):
        for h, cols in enumerate(head_cols):
            q = load(q_ref, b, cols)
            k_prev = kp_ref[:, cols] if b == 0 else load(k_ref, b - 1, cols)
            s_ref[b * ATT_HEADS + h, :, :ATT_BLOCK] = lax.dot_general(
                q, k_prev, contract_last, preferred_element_type=F32)
            s_ref[b * ATT_HEADS + h, :, ATT_BLOCK:] = lax.dot_general(
                q, load(k_ref, b, cols), contract_last, preferred_element_type=F32)

    for b in range(ATT_STEP):
        valid = first_band if b == 0 else band
        max_tile = jnp.zeros((ATT_BLOCK, LANES), F32)
        sum_tile = jnp.ones((ATT_BLOCK, LANES), F32)
        for h, cols in enumerate(head_cols):
            s = jnp.where(valid, s_ref[b * ATT_HEADS + h], -jnp.inf)
            m = jnp.max(s, axis=-1, keepdims=True)
            p = jnp.exp2(s - m)
            denom = jnp.sum(p, axis=-1, keepdims=True)
            pb = p.astype(BF16)
            v_prev = vp_ref[:, cols] if b == 0 else load(v_ref, b - 1, cols)
            pv = (jnp.dot(pb[:, :ATT_BLOCK], v_prev, preferred_element_type=F32)
                  + jnp.dot(pb[:, ATT_BLOCK:], load(v_ref, b, cols), preferred_element_type=F32))
            store(o_ref, b, cols, (pv / denom).astype(o_ref.dtype))
            max_tile = jnp.where(lane == h, m, max_tile)
            sum_tile = jnp.where(lane == h, denom, sum_tile)
        store(lse_ref, b, slice(None), (max_tile + jnp.log2(sum_tile)) * LN_2)

    last = ATT_STEP - 1
    kp_ref[...] = load(k_ref, last, slice(None))
    vp_ref[...] = load(v_ref, last, slice(None))


def _attention_group(qkv, group, dilation):
    n_sub = ATT_TILE // dilation
    step = ATT_STEP * ATT_BLOCK
    n_steps = SEQ // dilation // step
    n_tiles = TOKENS // ATT_TILE
    tiles_per_batch = SEQ // ATT_TILE
    q_unit, k_unit, v_unit = group, N_ATT_GROUPS + group, 2 * N_ATT_GROUPS + group
    view = qkv.reshape(n_tiles, dilation, n_sub, ATT_IN_WIDTH)

    if n_sub >= step:
        per_tile = n_sub // step
        lead = (None, None, step)
        index = lambda b, c, n: (b * tiles_per_batch + n // per_tile, c, n % per_tile)
    else:
        span = step // n_sub
        lead = (span, None, n_sub)
        index = lambda b, c, n: (b * (tiles_per_batch // span) + n, c, 0)

    def spec(width, unit):
        return pl.BlockSpec(lead + (width,), lambda b, c, n: index(b, c, n) + (unit,))

    return pl.pallas_call(
        _attention_kernel,
        grid=(BATCH, dilation, n_steps),
        in_specs=[spec(ATT_WIDTH, q_unit), spec(ATT_WIDTH, k_unit), spec(ATT_WIDTH, v_unit)],
        out_specs=[spec(ATT_WIDTH, 0), spec(LANES, 0)],
        out_shape=[
            jax.ShapeDtypeStruct((n_tiles, dilation, n_sub, ATT_WIDTH), BF16),
            jax.ShapeDtypeStruct((n_tiles, dilation, n_sub, LANES), F32),
        ],
        scratch_shapes=[
            pltpu.VMEM((ATT_BLOCK, ATT_WIDTH), BF16),
            pltpu.VMEM((ATT_BLOCK, ATT_WIDTH), BF16),
            pltpu.VMEM((ATT_STEP * ATT_HEADS, ATT_BLOCK, 2 * ATT_BLOCK), F32),
        ],
        compiler_params=_params(("parallel", "parallel", "arbitrary")),
        name=f"attention_d{dilation}",
    )(view, view, view)


OUT_TM = 256


def _att_outproj_kernel(o0_ref, o1_ref, o2_ref, l0_ref, l1_ref, l2_ref, p1_ref, p2_ref, w_ref,
                        res_ref, out_ref, lse_ref, even_ref, odd_ref):
    s = pl.program_id(0)

    @pl.when(s == 0)
    def _():
        odd_ref[...] = jnp.zeros_like(odd_ref)

    def step(merged_prev_ref, merged_ref):
        out_ref[...] = res_ref[...] + jnp.dot(merged_prev_ref[...], w_ref[...],
                                              preferred_element_type=F32)
        for k, l_ref in enumerate((l1_ref, l2_ref)):
            d, n = l_ref.shape[0], l_ref.shape[1]
            for c in range(d):
                lse_ref[k, pl.ds(c, n, stride=d), :] = l_ref[c]
        l0, l1, l2 = l0_ref[0], lse_ref[0], lse_ref[1]
        top = jnp.maximum(jnp.maximum(l0, l1), l2)
        e0, e1, e2 = jnp.exp(l0 - top), jnp.exp(l1 - top), jnp.exp(l2 - top)
        total = e0 + e1 + e2
        w0, w1, w2 = e0 / total, e1 / total, e2 / total
        o1 = jnp.dot(p1_ref[...], o1_ref[...].reshape(OUT_TM, ATT_WIDTH),
                     preferred_element_type=F32)
        o2 = jnp.dot(p2_ref[...], o2_ref[...].reshape(OUT_TM, ATT_WIDTH),
                     preferred_element_type=F32)
        for h in range(ATT_HEADS):
            cols = slice(h * ATT_HEAD_DIM, (h + 1) * ATT_HEAD_DIM)
            merged = (w0[:, h:h + 1] * o0_ref[0, :, cols].astype(F32)
                      + w1[:, h:h + 1] * o1[:, cols]
                      + w2[:, h:h + 1] * o2[:, cols])
            merged_ref[:, cols] = merged.astype(BF16)

    @pl.when(s % 2 == 0)
    def _():
        step(odd_ref, even_ref)

    @pl.when(s % 2 == 1)
    def _():
        step(even_ref, odd_ref)


def _token_order_matrix(dilation):
    t = jnp.arange(OUT_TM)
    src = (t % dilation) * (OUT_TM // dilation) + t // dilation
    return (src[:, None] == t[None, :]).astype(BF16)


def _att_outproj(outs, lses, w, res):
    tm = OUT_TM
    per_tile = ATT_TILE // tm

    n_row_tiles = TOKENS // tm
    merge_tile = lambda s: jnp.minimum(s, n_row_tiles - 1)
    project_tile = lambda s: jnp.maximum(s - 1, 0)

    def spec(dilation, width):
        return pl.BlockSpec(
            (None, dilation, tm // dilation, width),
            lambda s: (merge_tile(s) // per_tile, 0, merge_tile(s) % per_tile, 0))

    dils = [d for _, d in ATT_GROUPS]
    const = lambda shape: pl.BlockSpec(shape, lambda s: (0, 0))
    return pl.pallas_call(
        _att_outproj_kernel,
        grid=(n_row_tiles + 1,),
        in_specs=[spec(d, ATT_WIDTH) for d in dils] + [spec(d, LANES) for d in dils] + [
            const((tm, tm)), const((tm, tm)),
            pl.BlockSpec((None, ATT_WIDTH, D_MODEL), lambda s: (0, 0, 0)),
            pl.BlockSpec((tm, D_MODEL), lambda s: (project_tile(s), 0))],
        out_specs=pl.BlockSpec((tm, D_MODEL), lambda s: (project_tile(s), 0)),
        out_shape=jax.ShapeDtypeStruct((TOKENS, D_MODEL), F32),
        scratch_shapes=[pltpu.VMEM((N_ATT_GROUPS - 1, tm, LANES), F32),
                        pltpu.VMEM((tm, ATT_WIDTH), BF16),
                        pltpu.VMEM((tm, ATT_WIDTH), BF16)],
        compiler_params=_params(("arbitrary",)),
        name="att_outproj",
    )(*outs, *lses, _token_order_matrix(dils[1]), _token_order_matrix(dils[2]), w, res)


def _cos_sin(pos_lo, inv_freq):
    pos_hi = jnp.arange(SEQ // ATT_TILE, dtype=F32) * ATT_TILE
    ang_hi = pos_hi[:, None] * inv_freq
    ang_lo = pos_lo[..., None] * inv_freq
    ch, sh, cl, sl = lax.optimization_barrier(
        (jnp.cos(ang_hi), jnp.sin(ang_hi), jnp.cos(ang_lo), jnp.sin(ang_lo)))
    ch, sh = ch[:, None, :], sh[:, None, :]
    cl, sl = cl[..., None, :, :], sl[..., None, :, :]
    shape = pos_lo.shape[:-1] + (SEQ, inv_freq.shape[0])
    return (ch * cl - sh * sl).reshape(shape), (sh * cl + ch * sl).reshape(shape)


def _retention_tables():
    inv_freq = 1.0 / (RET_ROT_BASE ** jnp.linspace(0.0, 1.0, RET_DK // 2, dtype=F32))
    cos, sin = _cos_sin(jnp.arange(ATT_TILE, dtype=F32), inv_freq)
    log_gamma = jnp.log(1.0 - 2.0 ** (-5.0 - jnp.arange(RET_HEADS, dtype=F32)))
    idx = jnp.arange(RET_BLOCK, dtype=F32)
    pow_i, pow_neg_j = lax.optimization_barrier(
        (jnp.exp(log_gamma[:, None] * idx[None, :]), jnp.exp(-log_gamma[:, None] * idx[None, :])))
    inner_decay = jnp.where(idx[:, None] >= idx[None, :],
                            pow_i[:, :, None] * pow_neg_j[:, None, :], 0.0)
    query_decay = jnp.exp(log_gamma[:, None] * (idx[None, :] + 1.0))[:, :, None]
    key_decay = jnp.exp(log_gamma[:, None] * (RET_BLOCK - 1.0 - idx[None, :]))[:, :, None]
    chunk_decay = jnp.broadcast_to(jnp.exp(log_gamma * RET_BLOCK)[:, None, None],
                                   (RET_HEADS, 1, RET_DV))
    return cos, sin, inner_decay, query_decay, key_decay, chunk_decay


def _attention_tables():
    pos_lo = jnp.arange(ATT_TILE, dtype=F32)
    pos_lo = jnp.stack([pos_lo.reshape(ATT_TILE // d, d).T.reshape(ATT_TILE)
                        for _, d in ATT_GROUPS])
    inv_freq = ROPE_THETA ** (-jnp.arange(0, ROPE_DIM, 2, dtype=F32) / ROPE_DIM)
    inv_freq = jnp.concatenate([inv_freq, inv_freq, jnp.zeros((ATT_HEAD_DIM - ROPE_DIM,), F32)])
    return _cos_sin(pos_lo, inv_freq)


def kernel(x, norm_mix_gain, norm_mlp_gain, ret_w_in, ret_w_out, att_w_in, att_q_gain,
           att_k_gain, att_w_out, mlp_w_in, mlp_w_out):
    h = x.reshape(TOKENS, D_MODEL)

    cos, sin, inner_decay, query_decay, key_decay, chunk_decay = _retention_tables()
    proj, ret_wo, mlp0_w1, mlp0_w2 = _ret_inproj(
        h, norm_mix_gain[0:1], ret_w_in.astype(BF16), cos, sin,
        [(ret_w_out, 0), (mlp_w_in, 0), (mlp_w_out, 0)])
    y = _retention(proj, inner_decay, query_decay, key_decay, chunk_decay)
    h = _outproj(y, ret_wo, h)
    h, att_wi, att_wo, mlp1_w1, mlp1_w2 = _mlp(
        h, norm_mlp_gain[0:1], mlp0_w1, mlp0_w2,
        [(att_w_in, 0), (att_w_out, 0), (mlp_w_in, 1), (mlp_w_out, 1)])

    cos_full, sin_full = _attention_tables()
    qkv = _att_inproj(h, norm_mix_gain[1:2], att_wi, att_q_gain[0:1], att_k_gain[0:1],
                      cos_full, sin_full)
    outs, lses = [], []
    for g, (_, dilation) in enumerate(ATT_GROUPS):
        o, l = _attention_group(qkv, g, dilation)
        outs.append(o)
        lses.append(l)
    h = _att_outproj(outs, lses, att_wo, h)
    (h,) = _mlp(h, norm_mlp_gain[1:2], mlp1_w1, mlp1_w2)
    return h.reshape(BATCH, SEQ, D_MODEL)
```

```python
import functools

import jax
import jax.numpy as jnp
from jax import lax
from jax.experimental import pallas as pl
from jax.experimental.pallas import tpu as pltpu

F32 = jnp.float32
BF16 = jnp.bfloat16

D_MODEL = 2048
BATCH = 2
SEQ = 8192
TOKENS = BATCH * SEQ
EPS = 1e-6

RET_HEADS = 8
RET_DK = 256
RET_DV = 512
RET_QK_WIDTH = RET_HEADS * RET_DK
RET_V_WIDTH = RET_HEADS * RET_DV
RET_IN_WIDTH = 2 * RET_QK_WIDTH + 2 * RET_V_WIDTH
RET_CHUNK = 128
RET_ROT_BASE = 10000.0

ATT_GROUPS = ((128, 1), (512, 4), (2048, 16))
N_ATT_GROUPS = 3
ATT_HEAD_DIM = 128
ATT_HEADS = 16
ATT_WIDTH = ATT_HEADS * ATT_HEAD_DIM
ATT_IN_WIDTH = 3 * N_ATT_GROUPS * ATT_WIDTH
ATT_BLOCK = 128
ROPE_DIM = 32
ROPE_HALF = ROPE_DIM // 2
ROPE_THETA = 500000.0

D_FF = 4 * D_MODEL

LANES = 128
VMEM_LIMIT = 60 * 1024 * 1024


def _params(semantics):
    return pltpu.CompilerParams(dimension_semantics=semantics, vmem_limit_bytes=VMEM_LIMIT)


def _rms_norm_bf16(x, gain):
    ms = jnp.mean(x * x, axis=-1, keepdims=True)
    return (x * lax.rsqrt(ms + EPS) * gain).astype(BF16)


def _side_cast(weights, step_of, n_steps):
    bf16_sublanes = 16
    n_blocks = 1 << (n_steps.bit_length() - 1)
    for arr, _ in weights:
        n_blocks = min(n_blocks, arr.shape[1] // bf16_sublanes)
    operands, in_specs, out_specs, out_shapes = [], [], [], []
    for arr, layer in weights:
        _, rows, cols = arr.shape
        assert rows % (n_blocks * bf16_sublanes) == 0, (rows, n_blocks)
        block = (None, rows // n_blocks, cols)
        blk = lambda *ids: jnp.minimum(step_of(*ids), n_blocks - 1)
        operands.append(arr)
        in_specs.append(pl.BlockSpec(block, lambda *ids, layer=layer, blk=blk: (layer, blk(*ids), 0)))
        out_specs.append(pl.BlockSpec(block, lambda *ids, blk=blk: (0, blk(*ids), 0)))
        out_shapes.append(jax.ShapeDtypeStruct((1, rows, cols), BF16))
    return operands, in_specs, out_specs, out_shapes


def _run_side_casts(src_refs, dst_refs):
    for src, dst in zip(src_refs, dst_refs):
        dst[...] = src[...].astype(BF16)


RET_ROW_PARTS = 2


def _ret_inproj_kernel(h_ref, gain_ref, w_ref, cos_ref, sin_ref, *rest,
                       n_side, n_q_tiles, n_rot_tiles, n_plain_tiles, heads_per_tile):
    side_in, o_ref, side_out, hn_ref = (rest[:n_side], rest[n_side],
                                        rest[n_side + 1:2 * n_side + 1], rest[2 * n_side + 1])
    j = pl.program_id(1)

    @pl.when(j == 0)
    def _():
        hn_ref[...] = _rms_norm_bf16(h_ref[...], gain_ref[...])

    row_part = o_ref.shape[0] // RET_ROW_PARTS
    units = [(hh, r) for hh in range(heads_per_tile) for r in range(RET_ROW_PARTS)]

    def unit_dot(hh, r):
        return jnp.dot(hn_ref[r * row_part:(r + 1) * row_part, :],
                       w_ref[:, hh * RET_DK:(hh + 1) * RET_DK], preferred_element_type=F32)

    @pl.when((j >= n_rot_tiles) & (j < n_plain_tiles))
    def _():
        _run_side_casts(side_in, side_out)
        for hh, r in units:
            o_ref[r * row_part:(r + 1) * row_part, hh * RET_DK:(hh + 1) * RET_DK] = (
                unit_dot(hh, r).astype(o_ref.dtype))

    @pl.when(j >= n_plain_tiles)
    def _():
        _run_side_casts(side_in, side_out)
        for hh, r in units:
            gate = unit_dot(hh, r)
            half_gate = 0.5 * gate
            o_ref[r * row_part:(r + 1) * row_part, hh * RET_DK:(hh + 1) * RET_DK] = (
                half_gate + half_gate * jnp.tanh(half_gate)).astype(o_ref.dtype)

    @pl.when(j < n_rot_tiles)
    def _():
        _run_side_casts(side_in, side_out)
        scale = jnp.where(j >= n_q_tiles, RET_DK ** -0.5, 1.0).astype(F32)
        half = RET_DK // 2
        for hh, r in units:
            rows = slice(r * row_part, (r + 1) * row_part)
            c = cos_ref[rows, :] * scale
            s = sin_ref[rows, :] * scale
            acc = unit_dot(hh, r)
            lo = hh * RET_DK
            x1 = acc[:, :half]
            x2 = acc[:, half:]
            o_ref[rows, lo:lo + half] = (x1 * c - x2 * s).astype(o_ref.dtype)
            o_ref[rows, lo + half:lo + RET_DK] = (x2 * c + x1 * s).astype(o_ref.dtype)


def _ret_inproj(h, gain, w, cos, sin, side_weights, *, tm=1024, tn=2048):
    n_pos_blocks = SEQ // tm
    n_col_tiles = RET_IN_WIDTH // tn
    side_ops, side_in, side_out, side_shapes = _side_cast(
        side_weights, lambda i, j: i * n_col_tiles + j, (TOKENS // tm) * n_col_tiles)
    kern = functools.partial(
        _ret_inproj_kernel, n_side=len(side_ops), n_q_tiles=RET_QK_WIDTH // tn,
        n_rot_tiles=2 * RET_QK_WIDTH // tn,
        n_plain_tiles=(2 * RET_QK_WIDTH + RET_V_WIDTH) // tn, heads_per_tile=tn // RET_DK)
    return pl.pallas_call(
        kern,
        grid=(TOKENS // tm, n_col_tiles),
        in_specs=[
            pl.BlockSpec((tm, D_MODEL), lambda i, j: (i, 0)),
            pl.BlockSpec((1, D_MODEL), lambda i, j: (0, 0)),
            pl.BlockSpec((None, D_MODEL, tn), lambda i, j: (0, 0, j)),
            pl.BlockSpec((tm, RET_DK // 2), lambda i, j: (i % n_pos_blocks, 0)),
            pl.BlockSpec((tm, RET_DK // 2), lambda i, j: (i % n_pos_blocks, 0)),
        ] + side_in,
        out_specs=[pl.BlockSpec((tm, tn), lambda i, j: (i, j))] + side_out,
        out_shape=[jax.ShapeDtypeStruct((TOKENS, RET_IN_WIDTH), BF16)] + side_shapes,
        scratch_shapes=[pltpu.VMEM((tm, D_MODEL), BF16)],
        compiler_params=_params(("arbitrary", "arbitrary")),
        name="ret_inproj",
    )(h, gain, w, cos, sin, *side_ops)


RET_BLOCK = 256


def _retention_kernel(q_ref, k_ref, v_ref, g_ref, dec_ref, qd_ref, kd_ref, cd_ref, o_ref,
                      state_ref, acc_ref, upd_ref, *, n_chunks):
    @pl.when(pl.program_id(2) == 0)
    def _():
        state_ref[...] = jnp.zeros_like(state_ref)

    dec = dec_ref[...]
    qd = qd_ref[...]
    kd = kd_ref[...]
    cd = cd_ref[...]
    chunk = lambda c: slice(c * RET_BLOCK, (c + 1) * RET_BLOCK)

    for c in range(n_chunks):
        qc, kc, vc = q_ref[chunk(c), :], k_ref[chunk(c), :], v_ref[chunk(c), :]
        scores = lax.dot_general(qc, kc, (((1,), (1,)), ((), ())),
                                 preferred_element_type=F32) * dec
        acc_ref[chunk(c), :] = jnp.dot(scores.astype(BF16), vc, preferred_element_type=F32)
        k_dec = (kc.astype(F32) * kd).astype(BF16)
        upd_ref[c] = lax.dot_general(k_dec, vc, (((0,), (0,)), ((), ())),
                                     preferred_element_type=F32)

    for c in range(n_chunks):
        state = state_ref[...]
        cross = jnp.dot(q_ref[chunk(c), :], state.astype(BF16), preferred_element_type=F32) * qd
        state_ref[...] = state * cd + upd_ref[c]
        out = acc_ref[chunk(c), :] + cross
        ms = jnp.mean(out * out, axis=-1, keepdims=True)
        y = out * lax.rsqrt(ms + EPS) * g_ref[chunk(c), :].astype(F32)
        o_ref[chunk(c), :] = y.astype(o_ref.dtype)


def _retention(proj, dec, qd, kd, cd, *, rows=2048):
    n_row_blocks = SEQ // rows
    n_chunks = rows // RET_BLOCK
    k_off = RET_QK_WIDTH // RET_DK
    v_off = 2 * RET_QK_WIDTH // RET_DV
    g_off = v_off + RET_V_WIDTH // RET_DV
    row = lambda b, h, r: b * n_row_blocks + r
    kern = functools.partial(_retention_kernel, n_chunks=n_chunks)
    return pl.pallas_call(
        kern,
        grid=(BATCH, RET_HEADS, n_row_blocks),
        in_specs=[
            pl.BlockSpec((rows, RET_DK), lambda b, h, r: (row(b, h, r), h)),
            pl.BlockSpec((rows, RET_DK), lambda b, h, r: (row(b, h, r), k_off + h)),
            pl.BlockSpec((rows, RET_DV), lambda b, h, r: (row(b, h, r), v_off + h)),
            pl.BlockSpec((rows, RET_DV), lambda b, h, r: (row(b, h, r), g_off + h)),
            pl.BlockSpec((None, RET_BLOCK, RET_BLOCK), lambda b, h, r: (h, 0, 0)),
            pl.BlockSpec((None, RET_BLOCK, 1), lambda b, h, r: (h, 0, 0)),
            pl.BlockSpec((None, RET_BLOCK, 1), lambda b, h, r: (h, 0, 0)),
            pl.BlockSpec((None, 1, RET_DV), lambda b, h, r: (h, 0, 0)),
        ],
        out_specs=pl.BlockSpec((rows, RET_DV), lambda b, h, r: (row(b, h, r), h)),
        out_shape=jax.ShapeDtypeStruct((TOKENS, RET_V_WIDTH), BF16),
        scratch_shapes=[pltpu.VMEM((RET_DK, RET_DV), F32),
                        pltpu.VMEM((rows, RET_DV), F32),
                        pltpu.VMEM((n_chunks, RET_DK, RET_DV), F32)],
        compiler_params=_params(("parallel", "parallel", "arbitrary")),
        name="retention",
    )(proj, proj, proj, proj, dec, qd, kd, cd)


def _outproj_kernel(y_ref, w_ref, res_ref, o_ref):
    o_ref[...] = res_ref[...] + jnp.dot(y_ref[...], w_ref[...], preferred_element_type=F32)


def _outproj(y, w, res, *, tm=1024, tn=1024):
    k = y.shape[1]
    return pl.pallas_call(
        _outproj_kernel,
        grid=(D_MODEL // tn, TOKENS // tm),
        in_specs=[
            pl.BlockSpec((tm, k), lambda j, i: (i, 0)),
            pl.BlockSpec((None, k, tn), lambda j, i: (0, 0, j)),
            pl.BlockSpec((tm, tn), lambda j, i: (i, j)),
        ],
        out_specs=pl.BlockSpec((tm, tn), lambda j, i: (i, j)),
        out_shape=jax.ShapeDtypeStruct((TOKENS, D_MODEL), F32),
        compiler_params=_params(("parallel", "parallel")),
        name="outproj",
    )(y, w, res)


def _mlp_kernel(h_ref, gain_ref, w1_ref, w2_ref, *rest, n_side):
    side_in, o_ref, side_out, hn_ref = (rest[:n_side], rest[n_side],
                                        rest[n_side + 1:2 * n_side + 1], rest[2 * n_side + 1])
    @pl.when(pl.program_id(1) == 0)
    def _():
        h = h_ref[...]
        hn_ref[...] = _rms_norm_bf16(h, gain_ref[...])
        o_ref[...] = h

    _run_side_casts(side_in, side_out)
    a = jnp.dot(hn_ref[...], w1_ref[...], preferred_element_type=F32)
    act = jnp.square(jnp.maximum(a, 0.0)).astype(BF16)
    o_ref[...] += jnp.dot(act, w2_ref[...], preferred_element_type=F32)


def _mlp(h, gain, w1, w2, side_weights=(), *, tm=1024, tf=1024):
    n_ff_tiles = D_FF // tf
    side_ops, side_in, side_out, side_shapes = _side_cast(
        side_weights, lambda i, f: i * n_ff_tiles + f, (TOKENS // tm) * n_ff_tiles)
    return pl.pallas_call(
        functools.partial(_mlp_kernel, n_side=len(side_ops)),
        grid=(TOKENS // tm, n_ff_tiles),
        in_specs=[
            pl.BlockSpec((tm, D_MODEL), lambda i, f: (i, 0), pipeline_mode=pl.Buffered(1)),
            pl.BlockSpec((1, D_MODEL), lambda i, f: (0, 0)),
            pl.BlockSpec((None, D_MODEL, tf), lambda i, f: (0, 0, f)),
            pl.BlockSpec((None, tf, D_MODEL), lambda i, f: (0, f, 0)),
        ] + side_in,
        out_specs=[pl.BlockSpec((tm, D_MODEL), lambda i, f: (i, 0))] + side_out,
        out_shape=[jax.ShapeDtypeStruct((TOKENS, D_MODEL), F32)] + side_shapes,
        scratch_shapes=[pltpu.VMEM((tm, D_MODEL), BF16)],
        compiler_params=_params(("arbitrary", "arbitrary")),
        name="mlp",
    )(h, gain, w1, w2, *side_ops)


ATT_TILE = 256
ATT_SUB = 256
ATT_ROW_PARTS = 2


def _residue_major_matrix(dilation):
    r = jnp.arange(ATT_TILE)
    n = ATT_TILE // dilation
    src = (r % n) * dilation + r // n
    return (src[:, None] == r[None, :]).astype(BF16)


def _head_aux_matrix():
    i = jnp.arange(ATT_SUB)
    row, col = i[:, None], i[None, :]
    same_head = (row // ATT_HEAD_DIM) == (col // ATT_HEAD_DIM)
    r, c = row % ATT_HEAD_DIM, col % ATT_HEAD_DIM
    minus = same_head & (c < ROPE_HALF) & (r == c + ROPE_HALF)
    plus = same_head & (r < ROPE_HALF) & (c == r + ROPE_HALF)
    return (plus.astype(F32) - minus.astype(F32)).astype(BF16)


def _att_inproj_kernel(h_ref, gain_ref, w_ref, qg_ref, kg_ref, cos_ref, sin_ref, aux_ref,
                       p1_ref, p2_ref, o_ref, hn_ref, *, tiles_per_unit, n_q_tiles, n_qk_tiles):
    j = pl.program_id(1)
    tm, tn = o_ref.shape

    @pl.when(j == 0)
    def _():
        hn_ref[0] = _rms_norm_bf16(h_ref[...], gain_ref[...])
        for g, p_ref in ((1, p1_ref), (2, p2_ref)):
            for t in range(tm // ATT_TILE):
                rows = slice(t * ATT_TILE, (t + 1) * ATT_TILE)
                hn_ref[g, rows, :] = jnp.dot(p_ref[...], hn_ref[0, rows, :],
                                             preferred_element_type=F32).astype(BF16)

    g = (j // tiles_per_unit) % N_ATT_GROUPS

    row_part = tm // ATT_ROW_PARTS
    units = [(k, r) for r in range(ATT_ROW_PARTS) for k in range(tn // ATT_SUB)]

    def unit_dot(k, r):
        return jnp.dot(hn_ref[g, r * row_part:(r + 1) * row_part, :],
                       w_ref[:, k * ATT_SUB:(k + 1) * ATT_SUB], preferred_element_type=F32)

    @pl.when(j >= n_qk_tiles)
    def _():
        for k, r in units:
            o_ref[r * row_part:(r + 1) * row_part, k * ATT_SUB:(k + 1) * ATT_SUB] = (
                unit_dot(k, r).astype(o_ref.dtype))

    @pl.when(j < n_qk_tiles)
    def _():
        is_q = j < n_q_tiles
        head_gain = jnp.where(is_q, qg_ref[...] * (ATT_HEAD_DIM ** -0.5 * LOG2_E), kg_ref[...])

        def epilogue(k, r, acc):
            rows = slice(r * row_part, (r + 1) * row_part)
            c = cos_ref[rows, :]
            s = sin_ref[rows, :]
            xg = acc * jnp.concatenate([head_gain, head_gain], axis=1)
            partner = jnp.dot(xg.astype(BF16), aux_ref[...], preferred_element_type=F32)
            for hh in range(ATT_SUB // ATT_HEAD_DIM):
                cols = slice(hh * ATT_HEAD_DIM, (hh + 1) * ATT_HEAD_DIM)
                x = acc[:, cols]
                inv = lax.rsqrt(jnp.mean(x * x, axis=-1, keepdims=True) + EPS)
                rot = (xg[:, cols] * c + partner[:, cols] * s) * inv
                lo = k * ATT_SUB + hh * ATT_HEAD_DIM
                o_ref[rows, lo:lo + ATT_HEAD_DIM] = rot.astype(o_ref.dtype)

        acc = unit_dot(*units[0])
        for n, (k, r) in enumerate(units):
            nxt = unit_dot(*units[n + 1]) if n + 1 < len(units) else None
            epilogue(k, r, acc)
            acc = nxt


def _att_inproj(h, gain, w, q_gain, k_gain, cos, sin, *, tm=1024, tn=2048):
    n_pos_blocks = SEQ // tm
    group_width = N_ATT_GROUPS * ATT_WIDTH
    tiles_per_unit = ATT_WIDTH // tn
    kern = functools.partial(
        _att_inproj_kernel, tiles_per_unit=tiles_per_unit, n_q_tiles=group_width // tn,
        n_qk_tiles=2 * group_width // tn)
    pos_spec = pl.BlockSpec(
        (None, tm, ATT_HEAD_DIM),
        lambda i, j: ((j // tiles_per_unit) % N_ATT_GROUPS, i % n_pos_blocks, 0))
    return pl.pallas_call(
        kern,
        grid=(TOKENS // tm, ATT_IN_WIDTH // tn),
        in_specs=[
            pl.BlockSpec((tm, D_MODEL), lambda i, j: (i, 0)),
            pl.BlockSpec((1, D_MODEL), lambda i, j: (0, 0)),
            pl.BlockSpec((None, D_MODEL, tn), lambda i, j: (0, 0, j)),
            pl.BlockSpec((1, ATT_HEAD_DIM), lambda i, j: (0, 0)),
            pl.BlockSpec((1, ATT_HEAD_DIM), lambda i, j: (0, 0)),
            pos_spec, pos_spec,
            pl.BlockSpec((2 * ATT_HEAD_DIM, 2 * ATT_HEAD_DIM), lambda i, j: (0, 0)),
            pl.BlockSpec((ATT_TILE, ATT_TILE), lambda i, j: (0, 0)),
            pl.BlockSpec((ATT_TILE, ATT_TILE), lambda i, j: (0, 0)),
        ],
        out_specs=pl.BlockSpec((tm, tn), lambda i, j: (i, j)),
        out_shape=jax.ShapeDtypeStruct((TOKENS, ATT_IN_WIDTH), BF16),
        scratch_shapes=[pltpu.VMEM((N_ATT_GROUPS, tm, D_MODEL), BF16)],
        compiler_params=_params(("parallel", "arbitrary")),
        name="att_inproj",
    )(h, gain, w, q_gain, k_gain, cos, sin, _head_aux_matrix(),
      _residue_major_matrix(ATT_GROUPS[1][1]), _residue_major_matrix(ATT_GROUPS[2][1]))


ATT_STEP = 4
LOG2_E = 1.4426950408889634
LN_2 = 0.6931471805599453


def _attention_kernel(q_ref, k_ref, v_ref, o_ref, lse_ref, kp_ref, vp_ref, s_ref):
    n = pl.program_id(2)

    @pl.when(n == 0)
    def _():
        kp_ref[...] = jnp.zeros_like(kp_ref)
        vp_ref[...] = jnp.zeros_like(vp_ref)

    def block_rows(ref, b):
        if len(ref.shape) == 2:
            return (slice(b * ATT_BLOCK, (b + 1) * ATT_BLOCK),), (ATT_BLOCK,)
        n_sub = ref.shape[1]
        if n_sub >= ATT_BLOCK:
            tile, off = divmod(b * ATT_BLOCK, n_sub)
            return (tile, slice(off, off + ATT_BLOCK)), (ATT_BLOCK,)
        per = ATT_BLOCK // n_sub
        return (slice(b * per, (b + 1) * per), slice(None)), (per, n_sub)

    def load(ref, b, cols):
        idx, _ = block_rows(ref, b)
        x = ref[idx + (cols,)]
        return x.reshape(ATT_BLOCK, x.shape[-1])

    def store(ref, b, cols, value):
        idx, lead = block_rows(ref, b)
        ref[idx + (cols,)] = value.reshape(lead + (value.shape[-1],))

    qi = lax.broadcasted_iota(jnp.int32, (ATT_BLOCK, 2 * ATT_BLOCK), 0)
    kj = lax.broadcasted_iota(jnp.int32, (ATT_BLOCK, 2 * ATT_BLOCK), 1)
    band = (kj >= qi) & (kj <= qi + ATT_BLOCK)
    first_band = band & ((n > 0) | (kj >= ATT_BLOCK))
    lane = lax.broadcasted_iota(jnp.int32, (ATT_BLOCK, LANES), 1)
    contract_last = (((1,), (1,)), ((), ()))
    head_cols = [slice(h * ATT_HEAD_DIM, (h + 1) * ATT_HEAD_DIM) for h in range(ATT_HEADS)]

    for b in range(ATT_STEP):
        for h, cols in enumerate(head_cols):
            q = load(q_ref, b, cols)
            k_prev = kp_ref[:, cols] if b == 0 else load(k_ref, b - 1, cols)
            s_ref[b * ATT_HEADS + h, :, :ATT_BLOCK] = lax.dot_general(
                q, k_prev, contract_last, preferred_element_type=F32)
            s_ref[b * ATT_HEADS + h, :, ATT_BLOCK:] = lax.dot_general(
                q, load(k_ref, b, cols), contract_last, preferred_element_type=F32)

    for b in range(ATT_STEP):
        valid = first_band if b == 0 else band
        max_tile = jnp.zeros((ATT_BLOCK, LANES), F32)
        sum_tile = jnp.ones((ATT_BLOCK, LANES), F32)
        for h, cols in enumerate(head_cols):
            s = jnp.where(valid, s_ref[b * ATT_HEADS + h], -jnp.inf)
            m = jnp.max(s, axis=-1, keepdims=True)
            p = jnp.exp2(s - m)
            denom = jnp.sum(p, axis=-1, keepdims=True)
            pb = p.astype(BF16)
            v_prev = vp_ref[:, cols] if b == 0 else load(v_ref, b - 1, cols)
            pv = (jnp.dot(pb[:, :ATT_BLOCK], v_prev, preferred_element_type=F32)
                  + jnp.dot(pb[:, ATT_BLOCK:], load(v_ref, b, cols), preferred_element_type=F32))
            store(o_ref, b, cols, (pv / denom).astype(o_ref.dtype))
            max_tile = jnp.where(lane == h, m, max_tile)
            sum_tile = jnp.where(lane == h, denom, sum_tile)
        store(lse_ref, b, slice(None), (max_tile + jnp.log2(sum_tile)) * LN_2)

    last = ATT_STEP - 1
    kp_ref[...] = load(k_ref, last, slice(None))
    vp_ref[...] = load(v_ref, last, slice(None))


def _attention_group(qkv, group, dilation):
    n_sub = ATT_TILE // dilation
    step = ATT_STEP * ATT_BLOCK
    n_steps = SEQ // dilation // step
    n_tiles = TOKENS // ATT_TILE
    tiles_per_batch = SEQ // ATT_TILE
    q_unit, k_unit, v_unit = group, N_ATT_GROUPS + group, 2 * N_ATT_GROUPS + group
    view = qkv.reshape(n_tiles, dilation, n_sub, ATT_IN_WIDTH)

    if n_sub >= step:
        per_tile = n_sub // step
        lead = (None, None, step)
        index = lambda b, c, n: (b * tiles_per_batch + n // per_tile, c, n % per_tile)
    else:
        span = step // n_sub
        lead = (span, None, n_sub)
        index = lambda b, c, n: (b * (tiles_per_batch // span) + n, c, 0)

    def spec(width, unit):
        return pl.BlockSpec(lead + (width,), lambda b, c, n: index(b, c, n) + (unit,))

    return pl.pallas_call(
        _attention_kernel,
        grid=(BATCH, dilation, n_steps),
        in_specs=[spec(ATT_WIDTH, q_unit), spec(ATT_WIDTH, k_unit), spec(ATT_WIDTH, v_unit)],
        out_specs=[spec(ATT_WIDTH, 0), spec(LANES, 0)],
        out_shape=[
            jax.ShapeDtypeStruct((n_tiles, dilation, n_sub, ATT_WIDTH), BF16),
            jax.ShapeDtypeStruct((n_tiles, dilation, n_sub, LANES), F32),
        ],
        scratch_shapes=[
            pltpu.VMEM((ATT_BLOCK, ATT_WIDTH), BF16),
            pltpu.VMEM((ATT_BLOCK, ATT_WIDTH), BF16),
            pltpu.VMEM((ATT_STEP * ATT_HEADS, ATT_BLOCK, 2 * ATT_BLOCK), F32),
        ],
        compiler_params=_params(("parallel", "parallel", "arbitrary")),
        name=f"attention_d{dilation}",
    )(view, view, view)


OUT_TM = 256


def _att_outproj_kernel(o0_ref, o1_ref, o2_ref, l0_ref, l1_ref, l2_ref, p1_ref, p2_ref, w_ref,
                        res_ref, out_ref, lse_ref, even_ref, odd_ref):
    s = pl.program_id(0)

    @pl.when(s == 0)
    def _():
        odd_ref[...] = jnp.zeros_like(odd_ref)

    def step(merged_prev_ref, merged_ref):
        out_ref[...] = res_ref[...] + jnp.dot(merged_prev_ref[...], w_ref[...],
                                              preferred_element_type=F32)
        for k, l_ref in enumerate((l1_ref, l2_ref)):
            d, n = l_ref.shape[0], l_ref.shape[1]
            for c in range(d):
                lse_ref[k, pl.ds(c, n, stride=d), :] = l_ref[c]
        l0, l1, l2 = l0_ref[0], lse_ref[0], lse_ref[1]
        top = jnp.maximum(jnp.maximum(l0, l1), l2)
        e0, e1, e2 = jnp.exp(l0 - top), jnp.exp(l1 - top), jnp.exp(l2 - top)
        total = e0 + e1 + e2
        w0, w1, w2 = e0 / total, e1 / total, e2 / total
        o1 = jnp.dot(p1_ref[...], o1_ref[...].reshape(OUT_TM, ATT_WIDTH),
                     preferred_element_type=F32)
        o2 = jnp.dot(p2_ref[...], o2_ref[...].reshape(OUT_TM, ATT_WIDTH),
                     preferred_element_type=F32)
        for h in range(ATT_HEADS):
            cols = slice(h * ATT_HEAD_DIM, (h + 1) * ATT_HEAD_DIM)
            merged = (w0[:, h:h + 1] * o0_ref[0, :, cols].astype(F32)
                      + w1[:, h:h + 1] * o1[:, cols]
                      + w2[:, h:h + 1] * o2[:, cols])
            merged_ref[:, cols] = merged.astype(BF16)

    @pl.when(s % 2 == 0)
    def _():
        step(odd_ref, even_ref)

    @pl.when(s % 2 == 1)
    def _():
        step(even_ref, odd_ref)


def _token_order_matrix(dilation):
    t = jnp.arange(OUT_TM)
    src = (t % dilation) * (OUT_TM // dilation) + t // dilation
    return (src[:, None] == t[None, :]).astype(BF16)


def _att_outproj(outs, lses, w, res):
    tm = OUT_TM
    per_tile = ATT_TILE // tm

    n_row_tiles = TOKENS // tm
    merge_tile = lambda s: jnp.minimum(s, n_row_tiles - 1)
    project_tile = lambda s: jnp.maximum(s - 1, 0)

    def spec(dilation, width):
        return pl.BlockSpec(
            (None, dilation, tm // dilation, width),
            lambda s: (merge_tile(s) // per_tile, 0, merge_tile(s) % per_tile, 0))

    dils = [d for _, d in ATT_GROUPS]
    const = lambda shape: pl.BlockSpec(shape, lambda s: (0, 0))
    return pl.pallas_call(
        _att_outproj_kernel,
        grid=(n_row_tiles + 1,),
        in_specs=[spec(d, ATT_WIDTH) for d in dils] + [spec(d, LANES) for d in dils] + [
            const((tm, tm)), const((tm, tm)),
            pl.BlockSpec((None, ATT_WIDTH, D_MODEL), lambda s: (0, 0, 0)),
            pl.BlockSpec((tm, D_MODEL), lambda s: (project_tile(s), 0))],
        out_specs=pl.BlockSpec((tm, D_MODEL), lambda s: (project_tile(s), 0)),
        out_shape=jax.ShapeDtypeStruct((TOKENS, D_MODEL), F32),
        scratch_shapes=[pltpu.VMEM((N_ATT_GROUPS - 1, tm, LANES), F32),
                        pltpu.VMEM((tm, ATT_WIDTH), BF16),
                        pltpu.VMEM((tm, ATT_WIDTH), BF16)],
        compiler_params=_params(("arbitrary",)),
        name="att_outproj",
    )(*outs, *lses, _token_order_matrix(dils[1]), _token_order_matrix(dils[2]), w, res)


def _cos_sin(pos_lo, inv_freq):
    pos_hi = jnp.arange(SEQ // ATT_TILE, dtype=F32) * ATT_TILE
    ang_hi = pos_hi[:, None] * inv_freq
    ang_lo = pos_lo[..., None] * inv_freq
    ch, sh, cl, sl = lax.optimization_barrier(
        (jnp.cos(ang_hi), jnp.sin(ang_hi), jnp.cos(ang_lo), jnp.sin(ang_lo)))
    ch, sh = ch[:, None, :], sh[:, None, :]
    cl, sl = cl[..., None, :, :], sl[..., None, :, :]
    shape = pos_lo.shape[:-1] + (SEQ, inv_freq.shape[0])
    return (ch * cl - sh * sl).reshape(shape), (sh * cl + ch * sl).reshape(shape)


def _retention_tables():
    inv_freq = 1.0 / (RET_ROT_BASE ** jnp.linspace(0.0, 1.0, RET_DK // 2, dtype=F32))
    cos, sin = _cos_sin(jnp.arange(ATT_TILE, dtype=F32), inv_freq)
    log_gamma = jnp.log(1.0 - 2.0 ** (-5.0 - jnp.arange(RET_HEADS, dtype=F32)))
    idx = jnp.arange(RET_BLOCK, dtype=F32)
    pow_i, pow_neg_j = lax.optimization_barrier(
        (jnp.exp(log_gamma[:, None] * idx[None, :]), jnp.exp(-log_gamma[:, None] * idx[None, :])))
    inner_decay = jnp.where(idx[:, None] >= idx[None, :],
                            pow_i[:, :, None] * pow_neg_j[:, None, :], 0.0)
    query_decay = jnp.exp(log_gamma[:, None] * (idx[None, :] + 1.0))[:, :, None]
    key_decay = jnp.exp(log_gamma[:, None] * (RET_BLOCK - 1.0 - idx[None, :]))[:, :, None]
    chunk_decay = jnp.broadcast_to(jnp.exp(log_gamma * RET_BLOCK)[:, None, None],
                                   (RET_HEADS, 1, RET_DV))
    return cos, sin, inner_decay, query_decay, key_decay, chunk_decay


def _attention_tables():
    pos_lo = jnp.arange(ATT_TILE, dtype=F32)
    pos_lo = jnp.stack([pos_lo.reshape(ATT_TILE // d, d).T.reshape(ATT_TILE)
                        for _, d in ATT_GROUPS])
    inv_freq = ROPE_THETA ** (-jnp.arange(0, ROPE_DIM, 2, dtype=F32) / ROPE_DIM)
    inv_freq = jnp.concatenate([inv_freq, inv_freq, jnp.zeros((ATT_HEAD_DIM - ROPE_DIM,), F32)])
    return _cos_sin(pos_lo, inv_freq)


def kernel(x, norm_mix_gain, norm_mlp_gain, ret_w_in, ret_w_out, att_w_in, att_q_gain,
           att_k_gain, att_w_out, mlp_w_in, mlp_w_out):
    h = x.reshape(TOKENS, D_MODEL)

    cos, sin, inner_decay, query_decay, key_decay, chunk_decay = _retention_tables()
    proj, ret_wo, mlp0_w1, mlp0_w2 = _ret_inproj(
        h, norm_mix_gain[0:1], ret_w_in.astype(BF16), cos, sin,
        [(ret_w_out, 0), (mlp_w_in, 0), (mlp_w_out, 0)])
    y = _retention(proj, inner_decay, query_decay, key_decay, chunk_decay)
    h = _outproj(y, ret_wo, h)
    h, att_wi, att_wo, mlp1_w1, mlp1_w2 = _mlp(
        h, norm_mlp_gain[0:1], mlp0_w1, mlp0_w2,
        [(att_w_in, 0), (att_w_out, 0), (mlp_w_in, 1), (mlp_w_out, 1)])

    cos_full, sin_full = _attention_tables()
    qkv = _att_inproj(h, norm_mix_gain[1:2], att_wi, att_q_gain[0:1], att_k_gain[0:1],
                      cos_full, sin_full)
    outs, lses = [], []
    for g, (_, dilation) in enumerate(ATT_GROUPS):
        o, l = _attention_group(qkv, g, dilation)
        outs.append(o)
        lses.append(l)
    h = _att_outproj(outs, lses, att_wo, h)
    (h,) = _mlp(h, norm_mlp_gain[1:2], mlp1_w1, mlp1_w2)
    return h.reshape(BATCH, SEQ, D_MODEL)
```

```python
import functools

import jax
import jax.numpy as jnp
from jax import lax
from jax.experimental import pallas as pl
from jax.experimental.pallas import tpu as pltpu

F32 = jnp.float32
BF16 = jnp.bfloat16

D_MODEL = 2048
BATCH = 2
SEQ = 8192
TOKENS = BATCH * SEQ
EPS = 1e-6

RET_HEADS = 8
RET_DK = 256
RET_DV = 512
RET_QK_WIDTH = RET_HEADS * RET_DK
RET_V_WIDTH = RET_HEADS * RET_DV
RET_IN_WIDTH = 2 * RET_QK_WIDTH + 2 * RET_V_WIDTH
RET_CHUNK = 128
RET_ROT_BASE = 10000.0

ATT_GROUPS = ((128, 1), (512, 4), (2048, 16))
N_ATT_GROUPS = 3
ATT_HEAD_DIM = 128
ATT_HEADS = 16
ATT_WIDTH = ATT_HEADS * ATT_HEAD_DIM
ATT_IN_WIDTH = 3 * N_ATT_GROUPS * ATT_WIDTH
ATT_BLOCK = 128
ROPE_DIM = 32
ROPE_HALF = ROPE_DIM // 2
ROPE_THETA = 500000.0

D_FF = 4 * D_MODEL

LANES = 128
VMEM_LIMIT = 60 * 1024 * 1024


def _params(semantics):
    return pltpu.CompilerParams(dimension_semantics=semantics, vmem_limit_bytes=VMEM_LIMIT)


def _rms_norm_bf16(x, gain):
    ms = jnp.mean(x * x, axis=-1, keepdims=True)
    return (x * lax.rsqrt(ms + EPS) * gain).astype(BF16)


def _side_cast(weights, step_of, n_steps):
    bf16_sublanes = 16
    n_blocks = 1 << (n_steps.bit_length() - 1)
    for arr, _ in weights:
        n_blocks = min(n_blocks, arr.shape[1] // bf16_sublanes)
    operands, in_specs, out_specs, out_shapes = [], [], [], []
    for arr, layer in weights:
        _, rows, cols = arr.shape
        assert rows % (n_blocks * bf16_sublanes) == 0, (rows, n_blocks)
        block = (None, rows // n_blocks, cols)
        blk = lambda *ids: jnp.minimum(step_of(*ids), n_blocks - 1)
        operands.append(arr)
        in_specs.append(pl.BlockSpec(block, lambda *ids, layer=layer, blk=blk: (layer, blk(*ids), 0)))
        out_specs.append(pl.BlockSpec(block, lambda *ids, blk=blk: (0, blk(*ids), 0)))
        out_shapes.append(jax.ShapeDtypeStruct((1, rows, cols), BF16))
    return operands, in_specs, out_specs, out_shapes


def _run_side_casts(src_refs, dst_refs):
    for src, dst in zip(src_refs, dst_refs):
        dst[...] = src[...].astype(BF16)


RET_ROW_PARTS = 2


def _ret_inproj_kernel(h_ref, gain_ref, w_ref, cos_ref, sin_ref, *rest,
                       n_side, n_q_tiles, n_rot_tiles, n_plain_tiles, heads_per_tile):
    side_in, o_ref, side_out, hn_ref = (rest[:n_side], rest[n_side],
                                        rest[n_side + 1:2 * n_side + 1], rest[2 * n_side + 1])
    j = pl.program_id(1)

    @pl.when(j == 0)
    def _():
        hn_ref[...] = _rms_norm_bf16(h_ref[...], gain_ref[...])

    row_part = o_ref.shape[0] // RET_ROW_PARTS
    units = [(hh, r) for hh in range(heads_per_tile) for r in range(RET_ROW_PARTS)]

    def unit_dot(hh, r):
        return jnp.dot(hn_ref[r * row_part:(r + 1) * row_part, :],
                       w_ref[:, hh * RET_DK:(hh + 1) * RET_DK], preferred_element_type=F32)

    @pl.when((j >= n_rot_tiles) & (j < n_plain_tiles))
    def _():
        _run_side_casts(side_in, side_out)
        for hh, r in units:
            o_ref[r * row_part:(r + 1) * row_part, hh * RET_DK:(hh + 1) * RET_DK] = (
                unit_dot(hh, r).astype(o_ref.dtype))

    @pl.when(j >= n_plain_tiles)
    def _():
        _run_side_casts(side_in, side_out)
        for hh, r in units:
            gate = unit_dot(hh, r)
            half_gate = 0.5 * gate
            o_ref[r * row_part:(r + 1) * row_part, hh * RET_DK:(hh + 1) * RET_DK] = (
                half_gate + half_gate * jnp.tanh(half_gate)).astype(o_ref.dtype)

    @pl.when(j < n_rot_tiles)
    def _():
        _run_side_casts(side_in, side_out)
        scale = jnp.where(j >= n_q_tiles, RET_DK ** -0.5, 1.0).astype(F32)
        half = RET_DK // 2
        for hh, r in units:
            rows = slice(r * row_part, (r + 1) * row_part)
            c = cos_ref[rows, :] * scale
            s = sin_ref[rows, :] * scale
            acc = unit_dot(hh, r)
            lo = hh * RET_DK
            x1 = acc[:, :half]
            x2 = acc[:, half:]
            o_ref[rows, lo:lo + half] = (x1 * c - x2 * s).astype(o_ref.dtype)
            o_ref[rows, lo + half:lo + RET_DK] = (x2 * c + x1 * s).astype(o_ref.dtype)


def _ret_inproj(h, gain, w, cos, sin, side_weights, *, tm=1024, tn=2048):
    n_pos_blocks = SEQ // tm
    n_col_tiles = RET_IN_WIDTH // tn
    side_ops, side_in, side_out, side_shapes = _side_cast(
        side_weights, lambda i, j: i * n_col_tiles + j, (TOKENS // tm) * n_col_tiles)
    kern = functools.partial(
        _ret_inproj_kernel, n_side=len(side_ops), n_q_tiles=RET_QK_WIDTH // tn,
        n_rot_tiles=2 * RET_QK_WIDTH // tn,
        n_plain_tiles=(2 * RET_QK_WIDTH + RET_V_WIDTH) // tn, heads_per_tile=tn // RET_DK)
    return pl.pallas_call(
        kern,
        grid=(TOKENS // tm, n_col_tiles),
        in_specs=[
            pl.BlockSpec((tm, D_MODEL), lambda i, j: (i, 0)),
            pl.BlockSpec((1, D_MODEL), lambda i, j: (0, 0)),
            pl.BlockSpec((None, D_MODEL, tn), lambda i, j: (0, 0, j)),
            pl.BlockSpec((tm, RET_DK // 2), lambda i, j: (i % n_pos_blocks, 0)),
            pl.BlockSpec((tm, RET_DK // 2), lambda i, j: (i % n_pos_blocks, 0)),
        ] + side_in,
        out_specs=[pl.BlockSpec((tm, tn), lambda i, j: (i, j))] + side_out,
        out_shape=[jax.ShapeDtypeStruct((TOKENS, RET_IN_WIDTH), BF16)] + side_shapes,
        scratch_shapes=[pltpu.VMEM((tm, D_MODEL), BF16)],
        compiler_params=_params(("arbitrary", "arbitrary")),
        name="ret_inproj",
    )(h, gain, w, cos, sin, *side_ops)


RET_BLOCK = 256


def _retention_kernel(q_ref, k_ref, v_ref, g_ref, dec_ref, qd_ref, kd_ref, cd_ref, o_ref,
                      state_ref, acc_ref, upd_ref, *, n_chunks):
    @pl.when(pl.program_id(2) == 0)
    def _():
        state_ref[...] = jnp.zeros_like(state_ref)

    dec = dec_ref[...]
    qd = qd_ref[...]
    kd = kd_ref[...]
    cd = cd_ref[...]
    chunk = lambda c: slice(c * RET_BLOCK, (c + 1) * RET_BLOCK)

    for c in range(n_chunks):
        qc, kc, vc = q_ref[chunk(c), :], k_ref[chunk(c), :], v_ref[chunk(c), :]
        scores = lax.dot_general(qc, kc, (((1,), (1,)), ((), ())),
                                 preferred_element_type=F32) * dec
        acc_ref[chunk(c), :] = jnp.dot(scores.astype(BF16), vc, preferred_element_type=F32)
        k_dec = (kc.astype(F32) * kd).astype(BF16)
        upd_ref[c] = lax.dot_general(k_dec, vc, (((0,), (0,)), ((), ())),
                                     preferred_element_type=F32)

    for c in range(n_chunks):
        state = state_ref[...]
        cross = jnp.dot(q_ref[chunk(c), :], state.astype(BF16), preferred_element_type=F32) * qd
        state_ref[...] = state * cd + upd_ref[c]
        out = acc_ref[chunk(c), :] + cross
        ms = jnp.mean(out * out, axis=-1, keepdims=True)
        y = out * lax.rsqrt(ms + EPS) * g_ref[chunk(c), :].astype(F32)
        o_ref[chunk(c), :] = y.astype(o_ref.dtype)


def _retention(proj, dec, qd, kd, cd, *, rows=2048):
    n_row_blocks = SEQ // rows
    n_chunks = rows // RET_BLOCK
    k_off = RET_QK_WIDTH // RET_DK
    v_off = 2 * RET_QK_WIDTH // RET_DV
    g_off = v_off + RET_V_WIDTH // RET_DV
    row = lambda b, h, r: b * n_row_blocks + r
    kern = functools.partial(_retention_kernel, n_chunks=n_chunks)
    return pl.pallas_call(
        kern,
        grid=(BATCH, RET_HEADS, n_row_blocks),
        in_specs=[
            pl.BlockSpec((rows, RET_DK), lambda b, h, r: (row(b, h, r), h)),
            pl.BlockSpec((rows, RET_DK), lambda b, h, r: (row(b, h, r), k_off + h)),
            pl.BlockSpec((rows, RET_DV), lambda b, h, r: (row(b, h, r), v_off + h)),
            pl.BlockSpec((rows, RET_DV), lambda b, h, r: (row(b, h, r), g_off + h)),
            pl.BlockSpec((None, RET_BLOCK, RET_BLOCK), lambda b, h, r: (h, 0, 0)),
            pl.BlockSpec((None, RET_BLOCK, 1), lambda b, h, r: (h, 0, 0)),
            pl.BlockSpec((None, RET_BLOCK, 1), lambda b, h, r: (h, 0, 0)),
            pl.BlockSpec((None, 1, RET_DV), lambda b, h, r: (h, 0, 0)),
        ],
        out_specs=pl.BlockSpec((rows, RET_DV), lambda b, h, r: (row(b, h, r), h)),
        out_shape=jax.ShapeDtypeStruct((TOKENS, RET_V_WIDTH), BF16),
        scratch_shapes=[pltpu.VMEM((RET_DK, RET_DV), F32),
                        pltpu.VMEM((rows, RET_DV), F32),
                        pltpu.VMEM((n_chunks, RET_DK, RET_DV), F32)],
        compiler_params=_params(("parallel", "parallel", "arbitrary")),
        name="retention",
    )(proj, proj, proj, proj, dec, qd, kd, cd)


def _outproj_kernel(y_ref, w_ref, res_ref, o_ref):
    o_ref[...] = res_ref[...] + jnp.dot(y_ref[...], w_ref[...], preferred_element_type=F32)


def _outproj(y, w, res, *, tm=1024, tn=1024):
    k = y.shape[1]
    return pl.pallas_call(
        _outproj_kernel,
        grid=(D_MODEL // tn, TOKENS // tm),
        in_specs=[
            pl.BlockSpec((tm, k), lambda j, i: (i, 0)),
            pl.BlockSpec((None, k, tn), lambda j, i: (0, 0, j)),
            pl.BlockSpec((tm, tn), lambda j, i: (i, j)),
        ],
        out_specs=pl.BlockSpec((tm, tn), lambda j, i: (i, j)),
        out_shape=jax.ShapeDtypeStruct((TOKENS, D_MODEL), F32),
        compiler_params=_params(("parallel", "parallel")),
        name="outproj",
    )(y, w, res)


def _mlp_kernel(h_ref, gain_ref, w1_ref, w2_ref, *rest, n_side):
    side_in, o_ref, side_out, hn_ref = (rest[:n_side], rest[n_side],
                                        rest[n_side + 1:2 * n_side + 1], rest[2 * n_side + 1])
    @pl.when(pl.program_id(1) == 0)
    def _():
        h = h_ref[...]
        hn_ref[...] = _rms_norm_bf16(h, gain_ref[...])
        o_ref[...] = h

    _run_side_casts(side_in, side_out)
    a = jnp.dot(hn_ref[...], w1_ref[...], preferred_element_type=F32)
    act = jnp.square(jnp.maximum(a, 0.0)).astype(BF16)
    o_ref[...] += jnp.dot(act, w2_ref[...], preferred_element_type=F32)


def _mlp(h, gain, w1, w2, side_weights=(), *, tm=1024, tf=512):
    n_ff_tiles = D_FF // tf
    side_ops, side_in, side_out, side_shapes = _side_cast(
        side_weights, lambda i, f: i * n_ff_tiles + f, (TOKENS // tm) * n_ff_tiles)
    return pl.pallas_call(
        functools.partial(_mlp_kernel, n_side=len(side_ops)),
        grid=(TOKENS // tm, n_ff_tiles),
        in_specs=[
            pl.BlockSpec((tm, D_MODEL), lambda i, f: (i, 0)),
            pl.BlockSpec((1, D_MODEL), lambda i, f: (0, 0)),
            pl.BlockSpec((None, D_MODEL, tf), lambda i, f: (0, 0, f)),
            pl.BlockSpec((None, tf, D_MODEL), lambda i, f: (0, f, 0)),
        ] + side_in,
        out_specs=[pl.BlockSpec((tm, D_MODEL), lambda i, f: (i, 0))] + side_out,
        out_shape=[jax.ShapeDtypeStruct((TOKENS, D_MODEL), F32)] + side_shapes,
        scratch_shapes=[pltpu.VMEM((tm, D_MODEL), BF16)],
        compiler_params=_params(("arbitrary", "arbitrary")),
        name="mlp",
    )(h, gain, w1, w2, *side_ops)


ATT_TILE = 256
ATT_SUB = 256
ATT_ROW_PARTS = 2


def _residue_major_matrix(dilation):
    r = jnp.arange(ATT_TILE)
    n = ATT_TILE // dilation
    src = (r % n) * dilation + r // n
    return (src[:, None] == r[None, :]).astype(BF16)


def _head_aux_matrix():
    i = jnp.arange(ATT_SUB)
    row, col = i[:, None], i[None, :]
    same_head = (row // ATT_HEAD_DIM) == (col // ATT_HEAD_DIM)
    r, c = row % ATT_HEAD_DIM, col % ATT_HEAD_DIM
    minus = same_head & (c < ROPE_HALF) & (r == c + ROPE_HALF)
    plus = same_head & (r < ROPE_HALF) & (c == r + ROPE_HALF)
    return (plus.astype(F32) - minus.astype(F32)).astype(BF16)


def _att_inproj_kernel(h_ref, gain_ref, w_ref, qg_ref, kg_ref, cos_ref, sin_ref, aux_ref,
                       p1_ref, p2_ref, o_ref, hn_ref, *, tiles_per_unit, n_q_tiles, n_qk_tiles):
    j = pl.program_id(1)
    tm, tn = o_ref.shape

    @pl.when(j == 0)
    def _():
        hn_ref[0] = _rms_norm_bf16(h_ref[...], gain_ref[...])
        for g, p_ref in ((1, p1_ref), (2, p2_ref)):
            for t in range(tm // ATT_TILE):
                rows = slice(t * ATT_TILE, (t + 1) * ATT_TILE)
                hn_ref[g, rows, :] = jnp.dot(p_ref[...], hn_ref[0, rows, :],
                                             preferred_element_type=F32).astype(BF16)

    g = (j // tiles_per_unit) % N_ATT_GROUPS

    row_part = tm // ATT_ROW_PARTS
    units = [(k, r) for r in range(ATT_ROW_PARTS) for k in range(tn // ATT_SUB)]

    def unit_dot(k, r):
        return jnp.dot(hn_ref[g, r * row_part:(r + 1) * row_part, :],
                       w_ref[:, k * ATT_SUB:(k + 1) * ATT_SUB], preferred_element_type=F32)

    @pl.when(j >= n_qk_tiles)
    def _():
        for k, r in units:
            o_ref[r * row_part:(r + 1) * row_part, k * ATT_SUB:(k + 1) * ATT_SUB] = (
                unit_dot(k, r).astype(o_ref.dtype))

    @pl.when(j < n_qk_tiles)
    def _():
        is_q = j < n_q_tiles
        head_gain = jnp.where(is_q, qg_ref[...] * (ATT_HEAD_DIM ** -0.5 * LOG2_E), kg_ref[...])

        def epilogue(k, r, acc):
            rows = slice(r * row_part, (r + 1) * row_part)
            c = cos_ref[rows, :]
            s = sin_ref[rows, :]
            xg = acc * jnp.concatenate([head_gain, head_gain], axis=1)
            partner = jnp.dot(xg.astype(BF16), aux_ref[...], preferred_element_type=F32)
            for hh in range(ATT_SUB // ATT_HEAD_DIM):
                cols = slice(hh * ATT_HEAD_DIM, (hh + 1) * ATT_HEAD_DIM)
                x = acc[:, cols]
                inv = lax.rsqrt(jnp.mean(x * x, axis=-1, keepdims=True) + EPS)
                rot = (xg[:, cols] * c + partner[:, cols] * s) * inv
                lo = k * ATT_SUB + hh * ATT_HEAD_DIM
                o_ref[rows, lo:lo + ATT_HEAD_DIM] = rot.astype(o_ref.dtype)

        acc = unit_dot(*units[0])
        for n, (k, r) in enumerate(units):
            nxt = unit_dot(*units[n + 1]) if n + 1 < len(units) else None
            epilogue(k, r, acc)
            acc = nxt


def _att_inproj(h, gain, w, q_gain, k_gain, cos, sin, *, tm=1024, tn=2048):
    n_pos_blocks = SEQ // tm
    group_width = N_ATT_GROUPS * ATT_WIDTH
    tiles_per_unit = ATT_WIDTH // tn
    kern = functools.partial(
        _att_inproj_kernel, tiles_per_unit=tiles_per_unit, n_q_tiles=group_width // tn,
        n_qk_tiles=2 * group_width // tn)
    pos_spec = pl.BlockSpec(
        (None, tm, ATT_HEAD_DIM),
        lambda i, j: ((j // tiles_per_unit) % N_ATT_GROUPS, i % n_pos_blocks, 0))
    return pl.pallas_call(
        kern,
        grid=(TOKENS // tm, ATT_IN_WIDTH // tn),
        in_specs=[
            pl.BlockSpec((tm, D_MODEL), lambda i, j: (i, 0)),
            pl.BlockSpec((1, D_MODEL), lambda i, j: (0, 0)),
            pl.BlockSpec((None, D_MODEL, tn), lambda i, j: (0, 0, j)),
            pl.BlockSpec((1, ATT_HEAD_DIM), lambda i, j: (0, 0)),
            pl.BlockSpec((1, ATT_HEAD_DIM), lambda i, j: (0, 0)),
            pos_spec, pos_spec,
            pl.BlockSpec((2 * ATT_HEAD_DIM, 2 * ATT_HEAD_DIM), lambda i, j: (0, 0)),
            pl.BlockSpec((ATT_TILE, ATT_TILE), lambda i, j: (0, 0)),
            pl.BlockSpec((ATT_TILE, ATT_TILE), lambda i, j: (0, 0)),
        ],
        out_specs=pl.BlockSpec((tm, tn), lambda i, j: (i, j)),
        out_shape=jax.ShapeDtypeStruct((TOKENS, ATT_IN_WIDTH), BF16),
        scratch_shapes=[pltpu.VMEM((N_ATT_GROUPS, tm, D_MODEL), BF16)],
        compiler_params=_params(("parallel", "arbitrary")),
        name="att_inproj",
    )(h, gain, w, q_gain, k_gain, cos, sin, _head_aux_matrix(),
      _residue_major_matrix(ATT_GROUPS[1][1]), _residue_major_matrix(ATT_GROUPS[2][1]))


ATT_STEP = 4
LOG2_E = 1.4426950408889634
LN_2 = 0.6931471805599453


def _attention_kernel(q_ref, k_ref, v_ref, o_ref, lse_ref, kp_ref, vp_ref, s_ref):
    n = pl.program_id(2)

    @pl.when(n == 0)
    def _():
        kp_ref[...] = jnp.zeros_like(kp_ref)
        vp_ref[...] = jnp.zeros_like(vp_ref)

    def block_rows(ref, b):
        if len(ref.shape) == 2:
            return (slice(b * ATT_BLOCK, (b + 1) * ATT_BLOCK),), (ATT_BLOCK,)
        n_sub = ref.shape[1]
        if n_sub >= ATT_BLOCK:
            tile, off = divmod(b * ATT_BLOCK, n_sub)
            return (tile, slice(off, off + ATT_BLOCK)), (ATT_BLOCK,)
        per = ATT_BLOCK // n_sub
        return (slice(b * per, (b + 1) * per), slice(None)), (per, n_sub)

    def load(ref, b, cols):
        idx, _ = block_rows(ref, b)
        x = ref[idx + (cols,)]
        return x.reshape(ATT_BLOCK, x.shape[-1])

    def store(ref, b, cols, value):
        idx, lead = block_rows(ref, b)
        ref[idx + (cols,)] = value.reshape(lead + (value.shape[-1],))

    qi = lax.broadcasted_iota(jnp.int32, (ATT_BLOCK, 2 * ATT_BLOCK), 0)
    kj = lax.broadcasted_iota(jnp.int32, (ATT_BLOCK, 2 * ATT_BLOCK), 1)
    band = (kj >= qi) & (kj <= qi + ATT_BLOCK)
    first_band = band & ((n > 0) | (kj >= ATT_BLOCK))
    lane = lax.broadcasted_iota(jnp.int32, (ATT_BLOCK, LANES), 1)
    contract_last = (((1,), (1,)), ((), ()))
    head_cols = [slice(h * ATT_HEAD_DIM, (h + 1) * ATT_HEAD_DIM) for h in range(ATT_HEADS)]

    for b in range(ATT_STEP):
        for h, cols in enumerate(head_cols):
            q = load(q_ref, b, cols)
            k_prev = kp_ref[:, cols] if b == 0 else load(k_ref, b - 1, cols)
            s_ref[b * ATT_HEADS + h, :, :ATT_BLOCK] = lax.dot_general(
                q, k_prev, contract_last, preferred_element_type=F32)
            s_ref[b * ATT_HEADS + h, :, ATT_BLOCK:] = lax.dot_general(
                q, load(k_ref, b, cols), contract_last, preferred_element_type=F32)

    for b in range(ATT_STEP):
        valid = first_band if b == 0 else band
        max_tile = jnp.zeros((ATT_BLOCK, LANES), F32)
        sum_tile = jnp.ones((ATT_BLOCK, LANES), F32)
        for h, cols in enumerate(head_cols):
            s = jnp.where(valid, s_ref[b * ATT_HEADS + h], -jnp.inf)
            m = jnp.max(s, axis=-1, keepdims=True)
            p = jnp.exp2(s - m)
            denom = jnp.sum(p, axis=-1, keepdims=True)
            pb = p.astype(BF16)
            v_prev = vp_ref[:, cols] if b == 0 else load(v_ref, b - 1, cols)
            pv = (jnp.dot(pb[:, :ATT_BLOCK], v_prev, preferred_element_type=F32)
                  + jnp.dot(pb[:, ATT_BLOCK:], load(v_ref, b, cols), preferred_element_type=F32))
            store(o_ref, b, cols, (pv / denom).astype(o_ref.dtype))
            max_tile = jnp.where(lane == h, m, max_tile)
            sum_tile = jnp.where(lane == h, denom, sum_tile)
        store(lse_ref, b, slice(None), (max_tile + jnp.log2(sum_tile)) * LN_2)

    last = ATT_STEP - 1
    kp_ref[...] = load(k_ref, last, slice(None))
    vp_ref[...] = load(v_ref, last, slice(None))


def _attention_group(qkv, group, dilation):
    n_sub = ATT_TILE // dilation
    step = ATT_STEP * ATT_BLOCK
    n_steps = SEQ // dilation // step
    n_tiles = TOKENS // ATT_TILE
    tiles_per_batch = SEQ // ATT_TILE
    q_unit, k_unit, v_unit = group, N_ATT_GROUPS + group, 2 * N_ATT_GROUPS + group
    view = qkv.reshape(n_tiles, dilation, n_sub, ATT_IN_WIDTH)

    if n_sub >= step:
        per_tile = n_sub // step
        lead = (None, None, step)
        index = lambda b, c, n: (b * tiles_per_batch + n // per_tile, c, n % per_tile)
    else:
        span = step // n_sub
        lead = (span, None, n_sub)
        index = lambda b, c, n: (b * (tiles_per_batch // span) + n, c, 0)

    def spec(width, unit):
        return pl.BlockSpec(lead + (width,), lambda b, c, n: index(b, c, n) + (unit,))

    return pl.pallas_call(
        _attention_kernel,
        grid=(BATCH, dilation, n_steps),
        in_specs=[spec(ATT_WIDTH, q_unit), spec(ATT_WIDTH, k_unit), spec(ATT_WIDTH, v_unit)],
        out_specs=[spec(ATT_WIDTH, 0), spec(LANES, 0)],
        out_shape=[
            jax.ShapeDtypeStruct((n_tiles, dilation, n_sub, ATT_WIDTH), BF16),
            jax.ShapeDtypeStruct((n_tiles, dilation, n_sub, LANES), F32),
        ],
        scratch_shapes=[
            pltpu.VMEM((ATT_BLOCK, ATT_WIDTH), BF16),
            pltpu.VMEM((ATT_BLOCK, ATT_WIDTH), BF16),
            pltpu.VMEM((ATT_STEP * ATT_HEADS, ATT_BLOCK, 2 * ATT_BLOCK), F32),
        ],
        compiler_params=_params(("parallel", "parallel", "arbitrary")),
        name=f"attention_d{dilation}",
    )(view, view, view)


OUT_TM = 256


def _att_outproj_kernel(o0_ref, o1_ref, o2_ref, l0_ref, l1_ref, l2_ref, p1_ref, p2_ref, w_ref,
                        res_ref, out_ref, lse_ref, even_ref, odd_ref):
    s = pl.program_id(0)

    @pl.when(s == 0)
    def _():
        odd_ref[...] = jnp.zeros_like(odd_ref)

    def step(merged_prev_ref, merged_ref):
        out_ref[...] = res_ref[...] + jnp.dot(merged_prev_ref[...], w_ref[...],
                                              preferred_element_type=F32)
        for k, l_ref in enumerate((l1_ref, l2_ref)):
            d, n = l_ref.shape[0], l_ref.shape[1]
            for c in range(d):
                lse_ref[k, pl.ds(c, n, stride=d), :] = l_ref[c]
        l0, l1, l2 = l0_ref[0], lse_ref[0], lse_ref[1]
        top = jnp.maximum(jnp.maximum(l0, l1), l2)
        e0, e1, e2 = jnp.exp(l0 - top), jnp.exp(l1 - top), jnp.exp(l2 - top)
        total = e0 + e1 + e2
        w0, w1, w2 = e0 / total, e1 / total, e2 / total
        o1 = jnp.dot(p1_ref[...], o1_ref[...].reshape(OUT_TM, ATT_WIDTH),
                     preferred_element_type=F32)
        o2 = jnp.dot(p2_ref[...], o2_ref[...].reshape(OUT_TM, ATT_WIDTH),
                     preferred_element_type=F32)
        for h in range(ATT_HEADS):
            cols = slice(h * ATT_HEAD_DIM, (h + 1) * ATT_HEAD_DIM)
            merged = (w0[:, h:h + 1] * o0_ref[0, :, cols].astype(F32)
                      + w1[:, h:h + 1] * o1[:, cols]
                      + w2[:, h:h + 1] * o2[:, cols])
            merged_ref[:, cols] = merged.astype(BF16)

    @pl.when(s % 2 == 0)
    def _():
        step(odd_ref, even_ref)

    @pl.when(s % 2 == 1)
    def _():
        step(even_ref, odd_ref)


def _token_order_matrix(dilation):
    t = jnp.arange(OUT_TM)
    src = (t % dilation) * (OUT_TM // dilation) + t // dilation
    return (src[:, None] == t[None, :]).astype(BF16)


def _att_outproj(outs, lses, w, res):
    tm = OUT_TM
    per_tile = ATT_TILE // tm

    n_row_tiles = TOKENS // tm
    merge_tile = lambda s: jnp.minimum(s, n_row_tiles - 1)
    project_tile = lambda s: jnp.maximum(s - 1, 0)

    def spec(dilation, width):
        return pl.BlockSpec(
            (None, dilation, tm // dilation, width),
            lambda s: (merge_tile(s) // per_tile, 0, merge_tile(s) % per_tile, 0))

    dils = [d for _, d in ATT_GROUPS]
    const = lambda shape: pl.BlockSpec(shape, lambda s: (0, 0))
    return pl.pallas_call(
        _att_outproj_kernel,
        grid=(n_row_tiles + 1,),
        in_specs=[spec(d, ATT_WIDTH) for d in dils] + [spec(d, LANES) for d in dils] + [
            const((tm, tm)), const((tm, tm)),
            pl.BlockSpec((None, ATT_WIDTH, D_MODEL), lambda s: (0, 0, 0)),
            pl.BlockSpec((tm, D_MODEL), lambda s: (project_tile(s), 0))],
        out_specs=pl.BlockSpec((tm, D_MODEL), lambda s: (project_tile(s), 0)),
        out_shape=jax.ShapeDtypeStruct((TOKENS, D_MODEL), F32),
        scratch_shapes=[pltpu.VMEM((N_ATT_GROUPS - 1, tm, LANES), F32),
                        pltpu.VMEM((tm, ATT_WIDTH), BF16),
                        pltpu.VMEM((tm, ATT_WIDTH), BF16)],
        compiler_params=_params(("arbitrary",)),
        name="att_outproj",
    )(*outs, *lses, _token_order_matrix(dils[1]), _token_order_matrix(dils[2]), w, res)


def _cos_sin(pos_lo, inv_freq):
    pos_hi = jnp.arange(SEQ // ATT_TILE, dtype=F32) * ATT_TILE
    ang_hi = pos_hi[:, None] * inv_freq
    ang_lo = pos_lo[..., None] * inv_freq
    ch, sh, cl, sl = lax.optimization_barrier(
        (jnp.cos(ang_hi), jnp.sin(ang_hi), jnp.cos(ang_lo), jnp.sin(ang_lo)))
    ch, sh = ch[:, None, :], sh[:, None, :]
    cl, sl = cl[..., None, :, :], sl[..., None, :, :]
    shape = pos_lo.shape[:-1] + (SEQ, inv_freq.shape[0])
    return (ch * cl - sh * sl).reshape(shape), (sh * cl + ch * sl).reshape(shape)


def _retention_tables():
    inv_freq = 1.0 / (RET_ROT_BASE ** jnp.linspace(0.0, 1.0, RET_DK // 2, dtype=F32))
    cos, sin = _cos_sin(jnp.arange(ATT_TILE, dtype=F32), inv_freq)
    log_gamma = jnp.log(1.0 - 2.0 ** (-5.0 - jnp.arange(RET_HEADS, dtype=F32)))
    idx = jnp.arange(RET_BLOCK, dtype=F32)
    pow_i, pow_neg_j = lax.optimization_barrier(
        (jnp.exp(log_gamma[:, None] * idx[None, :]), jnp.exp(-log_gamma[:, None] * idx[None, :])))
    inner_decay = jnp.where(idx[:, None] >= idx[None, :],
                            pow_i[:, :, None] * pow_neg_j[:, None, :], 0.0)
    query_decay = jnp.exp(log_gamma[:, None] * (idx[None, :] + 1.0))[:, :, None]
    key_decay = jnp.exp(log_gamma[:, None] * (RET_BLOCK - 1.0 - idx[None, :]))[:, :, None]
    chunk_decay = jnp.broadcast_to(jnp.exp(log_gamma * RET_BLOCK)[:, None, None],
                                   (RET_HEADS, 1, RET_DV))
    return cos, sin, inner_decay, query_decay, key_decay, chunk_decay


def _attention_tables():
    pos_lo = jnp.arange(ATT_TILE, dtype=F32)
    pos_lo = jnp.stack([pos_lo.reshape(ATT_TILE // d, d).T.reshape(ATT_TILE)
                        for _, d in ATT_GROUPS])
    inv_freq = ROPE_THETA ** (-jnp.arange(0, ROPE_DIM, 2, dtype=F32) / ROPE_DIM)
    inv_freq = jnp.concatenate([inv_freq, inv_freq, jnp.zeros((ATT_HEAD_DIM - ROPE_DIM,), F32)])
    return _cos_sin(pos_lo, inv_freq)


def kernel(x, norm_mix_gain, norm_mlp_gain, ret_w_in, ret_w_out, att_w_in, att_q_gain,
           att_k_gain, att_w_out, mlp_w_in, mlp_w_out):
    h = x.reshape(TOKENS, D_MODEL)

    cos, sin, inner_decay, query_decay, key_decay, chunk_decay = _retention_tables()
    proj, ret_wo, mlp0_w1, mlp0_w2 = _ret_inproj(
        h, norm_mix_gain[0:1], ret_w_in.astype(BF16), cos, sin,
        [(ret_w_out, 0), (mlp_w_in, 0), (mlp_w_out, 0)])
    y = _retention(proj, inner_decay, query_decay, key_decay, chunk_decay)
    h = _outproj(y, ret_wo, h)
    h, att_wi, att_wo, mlp1_w1, mlp1_w2 = _mlp(
        h, norm_mlp_gain[0:1], mlp0_w1, mlp0_w2,
        [(att_w_in, 0), (att_w_out, 0), (mlp_w_in, 1), (mlp_w_out, 1)])

    cos_full, sin_full = _attention_tables()
    qkv = _att_inproj(h, norm_mix_gain[1:2], att_wi, att_q_gain[0:1], att_k_gain[0:1],
                      cos_full, sin_full)
    outs, lses = [], []
    for g, (_, dilation) in enumerate(ATT_GROUPS):
        o, l = _attention_group(qkv, g, dilation)
        outs.append(o)
        lses.append(l)
    h = _att_outproj(outs, lses, att_wo, h)
    (h,) = _mlp(h, norm_mlp_gain[1:2], mlp1_w1, mlp1_w2)
    return h.reshape(BATCH, SEQ, D_MODEL)
```
